```python
import jax, jax.numpy as jnp
from jax import lax
import numpy as np

D_MODEL = 1024
BATCH = 16
SEQ = 2048
DEPTH = 2

D_MIX = D_MODEL
HEAD_DIM = 64
CONV_WIDTH = 256
SB_HEADS = 6
SB_WIDTH = SB_HEADS * HEAD_DIM
RWKV_HEADS = 6
RWKV_WIDTH = RWKV_HEADS * HEAD_DIM
CONV_TAPS = 31
DECAY_LORA = 32
ICLR_LORA = 32
VRES_LORA = 32
BLOCK_Q = 128
RMS_EPS = 1e-6
LN_EPS = 1e-5
GN_EPS = 64e-5
IN_WIDTHS = (CONV_WIDTH, CONV_WIDTH, CONV_WIDTH,
             SB_WIDTH, SB_WIDTH, SB_WIDTH, SB_WIDTH,
             RWKV_WIDTH, RWKV_WIDTH, RWKV_WIDTH,
             DECAY_LORA, ICLR_LORA,
             RWKV_WIDTH)
D_IN = 3 * CONV_WIDTH + 4 * SB_WIDTH + 4 * RWKV_WIDTH + DECAY_LORA + ICLR_LORA
SHIFT_WIDTH = 3 * RWKV_WIDTH + DECAY_LORA + ICLR_LORA

kernel_name = 'hybrid_conformer_stickbreak_rwkv7'


def _split_cols(p, widths):
    idx, acc = [], 0
    for w in widths[:-1]:
        acc += w
        idx.append(acc)
    return jnp.split(p, idx, axis=-1)


def rms_norm(x, g):
    xf = x.astype(jnp.float32)
    y = xf * lax.rsqrt(jnp.mean(xf * xf, axis=-1, keepdims=True) + RMS_EPS)
    return (y * g.astype(jnp.float32)).astype(x.dtype)


def token_shift(f):
    return jnp.pad(f, ((0, 0), (1, 0), (0, 0)))[:, :-1]


def _heads(t, n_heads):
    b, s, _ = t.shape
    return t.reshape(b, s, n_heads, HEAD_DIM).transpose(0, 2, 1, 3)


def conformer_conv(glu_a, glu_b, dw, dw_b, ln_g, ln_b, pw, pw_b):
    u = glu_a * jax.nn.sigmoid(glu_b)
    u = lax.conv_general_dilated(u, dw[:, None, :].astype(u.dtype), window_strides=(1,),
                                 padding=((CONV_TAPS - 1, 0),),
                                 dimension_numbers=('NWC', 'WIO', 'NWC'),
                                 feature_group_count=CONV_WIDTH) + dw_b
    uf = u.astype(jnp.float32)
    mean = jnp.mean(uf, axis=-1, keepdims=True)
    var = jnp.mean(jnp.square(uf - mean), axis=-1, keepdims=True)
    uf = (uf - mean) * lax.rsqrt(var + LN_EPS) * ln_g + ln_b
    uf = jax.nn.silu(uf)
    return uf.astype(glu_a.dtype) @ pw + pw_b


def stick_breaking_attention(q, k, v):
    b, h, t_len, dh = q.shape
    scale = HEAD_DIM ** -0.5
    outs = []
    for blk in range(t_len // BLOCK_Q):
        q0, q1 = blk * BLOCK_Q, (blk + 1) * BLOCK_Q
        z = jnp.einsum('bhqd,bhkd->bhqk', q[:, :, q0:q1], k[:, :, :q1]).astype(jnp.float32) * scale
        causal = jnp.arange(q1)[None, :] < jnp.arange(q0, q1)[:, None]
        log_keep = jnp.where(causal, jax.nn.log_sigmoid(-z), 0.0)
        log_rest = lax.cumsum(log_keep, axis=3, reverse=True) - log_keep
        att = jnp.where(causal, jnp.exp(jax.nn.log_sigmoid(z) + log_rest), 0.0)
        outs.append(jnp.einsum('bhqk,bhkd->bhqd', att.astype(v.dtype), v[:, :, :q1]))
    o = jnp.concatenate(outs, axis=2)
    return o.transpose(0, 2, 1, 3).reshape(b, t_len, h * dh)


def rwkv7_scan(r, w, k, v, a, b):
    bsz, _, h, n = r.shape

    def step(S, inp):
        r_t, w_t, k_t, v_t, a_t, b_t = inp
        sa = jnp.einsum('bhvk,bhk->bhv', S, a_t)
        S = S * w_t[:, :, None, :] + sa[..., None] * b_t[:, :, None, :] + v_t[..., None] * k_t[:, :, None, :]
        return S, jnp.einsum('bhvk,bhk->bhv', S, r_t)

    S0 = jnp.zeros((bsz, h, n, n), jnp.float32)
    xs = tuple(t.transpose(1, 0, 2, 3) for t in (r, w, k, v, a, b))
    _, y = lax.scan(step, S0, xs)
    return y.transpose(1, 0, 2, 3)


def rwkv7_time_mix(r, k, v, w_d, a_d, mu, w0, w2, a0, a2, kk_scale, ka, rk, gn_g, gn_b, v_first, v_res):
    out_dtype = r.dtype
    feats = jnp.concatenate([r, k, v, w_d, a_d], axis=-1)
    feats = feats + (token_shift(feats) - feats) * mu
    r, k, v, w_d, a_d = _split_cols(feats.astype(jnp.float32), (RWKV_WIDTH, RWKV_WIDTH, RWKV_WIDTH, DECAY_LORA, ICLR_LORA))
    w_log = -jax.nn.softplus(-(w0 + jnp.tanh(w_d) @ w2)) - 0.5
    decay = jnp.exp(-jnp.exp(w_log))
    a = jax.nn.sigmoid(a0 + a_d @ a2)
    if v_res is None:
        v_first = v
    else:
        v0, v1, v2 = v_res
        v = v + (v_first - v) * jax.nn.sigmoid(v0 + (v @ v1) @ v2)
    bsz, t_len, _ = r.shape
    hd = lambda t: t.reshape(bsz, t_len, RWKV_HEADS, HEAD_DIM)
    kk = hd(k * kk_scale)
    kk = kk / jnp.maximum(jnp.sqrt(jnp.sum(kk * kk, axis=-1, keepdims=True)), 1e-12)
    k = k * (1.0 + (a - 1.0) * ka)
    rh, kh, vh, ah = hd(r), hd(k), hd(v), hd(a)
    y = rwkv7_scan(rh, hd(decay), kh, vh, -kk, kk * ah)
    mean = jnp.mean(y, axis=-1, keepdims=True)
    var = jnp.mean(jnp.square(y - mean), axis=-1, keepdims=True)
    y = ((y - mean) * lax.rsqrt(var + GN_EPS)).reshape(bsz, t_len, RWKV_WIDTH) * gn_g + gn_b
    y = y + (jnp.sum(rh * kh * rk, axis=-1, keepdims=True) * vh).reshape(bsz, t_len, RWKV_WIDTH)
    return y.astype(out_dtype), v_first


def setup_inputs(seed: int = 0) -> dict:
    key = jax.random.key(seed)
    ks = jax.random.split(key, 24)
    L = DEPTH
    nrm = lambda k, shape, s: jax.random.normal(k, shape, jnp.float32) * s
    return {
        'x': nrm(ks[0], (BATCH, SEQ, D_MODEL), 1.0),
        'pre_norm_g': 1.0 + nrm(ks[1], (L, D_MODEL), 0.02),
        'post_norm_g': 1.0 + nrm(ks[2], (L, D_MODEL), 0.02),
        'w_in': nrm(ks[3], (L, D_MODEL, D_IN), D_MODEL ** -0.5),
        'w_out': nrm(ks[4], (L, D_MIX, D_MODEL), D_MIX ** -0.5),
        'conv_dw': nrm(ks[5], (L, CONV_TAPS, CONV_WIDTH), CONV_TAPS ** -0.5),
        'conv_dw_b': nrm(ks[6], (L, CONV_WIDTH), 0.02),
        'conv_ln_g': 1.0 + nrm(ks[7], (L, CONV_WIDTH), 0.02),
        'conv_ln_b': nrm(ks[8], (L, CONV_WIDTH), 0.02),
        'conv_pw': nrm(ks[9], (L, CONV_WIDTH, CONV_WIDTH), CONV_WIDTH ** -0.5),
        'conv_pw_b': nrm(ks[10], (L, CONV_WIDTH), 0.02),
        'rwkv_mu': jax.random.uniform(ks[11], (L, SHIFT_WIDTH), jnp.float32),
        'rwkv_w0': jax.random.uniform(ks[12], (L, RWKV_WIDTH), jnp.float32, minval=-4.0, maxval=0.0),
        'rwkv_w2': nrm(ks[13], (L, DECAY_LORA, RWKV_WIDTH), 0.1),
        'rwkv_a0': nrm(ks[14], (L, RWKV_WIDTH), 0.1),
        'rwkv_a2': nrm(ks[15], (L, ICLR_LORA, RWKV_WIDTH), 0.1),
        'rwkv_kk_scale': 0.85 + nrm(ks[16], (L, RWKV_WIDTH), 0.05),
        'rwkv_ka': 1.0 + nrm(ks[17], (L, RWKV_WIDTH), 0.05),
        'rwkv_rk': nrm(ks[18], (L, RWKV_HEADS, HEAD_DIM), 0.1),
        'rwkv_gn_g': 1.0 + nrm(ks[19], (L, RWKV_WIDTH), 0.02),
        'rwkv_gn_b': nrm(ks[20], (L, RWKV_WIDTH), 0.02),
        'rwkv_v0': nrm(ks[21], (L - 1, RWKV_WIDTH), 0.1),
        'rwkv_v1': nrm(ks[22], (L - 1, RWKV_WIDTH, VRES_LORA), RWKV_WIDTH ** -0.5),
        'rwkv_v2': nrm(ks[23], (L - 1, VRES_LORA, RWKV_WIDTH), 0.1),
    }


def reference(x, pre_norm_g, post_norm_g, w_in, w_out, conv_dw, conv_dw_b, conv_ln_g, conv_ln_b,
              conv_pw, conv_pw_b, rwkv_mu, rwkv_w0, rwkv_w2, rwkv_a0, rwkv_a2, rwkv_kk_scale, rwkv_ka,
              rwkv_rk, rwkv_gn_g, rwkv_gn_b, rwkv_v0, rwkv_v1, rwkv_v2):
    v_first = None
    for l in range(DEPTH):
        h = rms_norm(x, pre_norm_g[l])
        p = h @ w_in[l]
        (glu_a, glu_b, g_conv, q, k, v, g_sb, r_r, r_k, r_v, w_d, a_d, g_rwkv) = _split_cols(p, IN_WIDTHS)
        y_conv = conformer_conv(glu_a, glu_b, conv_dw[l], conv_dw_b[l], conv_ln_g[l], conv_ln_b[l],
                                conv_pw[l], conv_pw_b[l])
        y_sb = stick_breaking_attention(_heads(q, SB_HEADS), _heads(k, SB_HEADS), _heads(v, SB_HEADS))
        v_res = None if l == 0 else (rwkv_v0[l - 1], rwkv_v1[l - 1], rwkv_v2[l - 1])
        y_rwkv, v_first = rwkv7_time_mix(r_r, r_k, r_v, w_d, a_d, rwkv_mu[l], rwkv_w0[l], rwkv_w2[l],
                                         rwkv_a0[l], rwkv_a2[l], rwkv_kk_scale[l], rwkv_ka[l], rwkv_rk[l],
                                         rwkv_gn_g[l], rwkv_gn_b[l], v_first, v_res)
        mix = jnp.concatenate([y_conv * jax.nn.silu(g_conv),
                               y_sb * jax.nn.silu(g_sb),
                               y_rwkv * jax.nn.silu(g_rwkv)], axis=-1)
        x = x + rms_norm(mix @ w_out[l], post_norm_g[l])
    return x
```

```python
import functools

import jax
import jax.numpy as jnp
from jax import lax
from jax.experimental import pallas as pl
from jax.experimental.pallas import tpu as pltpu

D_MODEL = 1024
HEAD_DIM = 64
CONV_WIDTH = 256
SB_WIDTH = 384
RWKV_WIDTH = 384
CONV_TAPS = 31
LORA = 32
BLOCK_Q = 128
RMS_EPS = 1e-6
LN_EPS = 1e-5
GN_EPS = 64e-5

LANES = 128
PAIRS = RWKV_WIDTH // LANES
CHUNK = 64
WA_WIDTH = LANES
R_IN = 3 * RWKV_WIDTH + WA_WIDTH + RWKV_WIDTH
SHIFT_COLS = 3 * RWKV_WIDTH + WA_WIDTH
VMEM_LIMIT = 56 * 1024 * 1024

F32 = jnp.float32
BF16 = jnp.bfloat16


def _dot1(a, b):
    return jnp.dot(a.astype(BF16), b.astype(BF16), preferred_element_type=F32)


def _split(a):
    hi = a.astype(BF16)
    lo = (a - hi.astype(F32)).astype(BF16)
    return hi, lo


def _dot3(a, b):
    ah, al = _split(a)
    bh, bl = _split(b)
    d = functools.partial(jnp.dot, preferred_element_type=F32)
    return d(ah, bh) + (d(ah, bl) + d(al, bh))


def _dot2_exact_rhs(a, b_bf16):
    ah, al = _split(a)
    d = functools.partial(jnp.dot, preferred_element_type=F32)
    return d(ah, b_bf16) + d(al, b_bf16)


def _dot_nt1(a, b):
    return lax.dot_general(a.astype(BF16), b.astype(BF16), (((1,), (1,)), ((), ())),
                           preferred_element_type=F32)


def _sigmoid(x):
    return 1.0 / (1.0 + jnp.exp(-x))


def _silu(x):
    return x * _sigmoid(x)


def _softplus(x):
    return jnp.maximum(x, 0.0) + jnp.log1p(jnp.exp(-jnp.abs(x)))


def _inproj_kernel(x_ref, g_ref, wc_ref, ws_ref, wr_ref, oc_ref, os_ref, or_ref):
    x = x_ref[...]
    ms = jnp.mean(x * x, axis=-1, keepdims=True)
    h = (x * lax.rsqrt(ms + RMS_EPS) * g_ref[...]).astype(BF16)
    for w_ref, o_ref in ((wc_ref, oc_ref), (ws_ref, os_ref), (wr_ref, or_ref)):
        n = o_ref.shape[1]
        for c0 in range(0, n, 256):
            c1 = min(c0 + 256, n)
            o_ref[:, c0:c1] = jnp.dot(h, w_ref[:, c0:c1], preferred_element_type=F32)


def _inproj(x2, g, wc, ws, wr, tm):
    n = x2.shape[0]
    full = lambda a: pl.BlockSpec(a.shape, lambda i: (0, 0))
    return pl.pallas_call(
        _inproj_kernel,
        grid=(n // tm,),
        in_specs=[pl.BlockSpec((tm, D_MODEL), lambda i: (i, 0)), full(g), full(wc), full(ws), full(wr)],
        out_specs=[pl.BlockSpec((tm, wc.shape[1]), lambda i: (i, 0)),
                   pl.BlockSpec((tm, ws.shape[1]), lambda i: (i, 0)),
                   pl.BlockSpec((tm, wr.shape[1]), lambda i: (i, 0))],
        out_shape=[jax.ShapeDtypeStruct((n, wc.shape[1]), F32),
                   jax.ShapeDtypeStruct((n, ws.shape[1]), F32),
                   jax.ShapeDtypeStruct((n, wr.shape[1]), F32)],
        compiler_params=pltpu.CompilerParams(dimension_semantics=("parallel",),
                                             vmem_limit_bytes=VMEM_LIMIT),
        name="inproj",
    )(x2, g, wc, ws, wr)


CONV_HALO = 32
CONV_SUB = 64


def _conv_kernel(pc_ref, dw_ref, dwb_ref, lng_ref, lnb_ref, pw_ref, pwb_ref, o_ref, u_ref):
    tt = o_ref.shape[0]

    @pl.when(pl.program_id(1) == 0)
    def _():
        u_ref[0:CONV_HALO, :] = jnp.zeros((CONV_HALO, CONV_WIDTH), F32)

    u_ref[CONV_HALO:CONV_HALO + tt, :] = pc_ref[:, 0:CONV_WIDTH] * _sigmoid(pc_ref[:, CONV_WIDTH:2 * CONV_WIDTH])
    for r0 in range(0, tt, CONV_SUB):
        acc = jnp.broadcast_to(dwb_ref[...], (CONV_SUB, CONV_WIDTH))
        for tap in range(CONV_TAPS):
            s = CONV_HALO + r0 - (CONV_TAPS - 1) + tap
            acc = acc + dw_ref[tap:tap + 1, :] * u_ref[s:s + CONV_SUB, :]
        mean = jnp.mean(acc, axis=-1, keepdims=True)
        d = acc - mean
        var = jnp.mean(d * d, axis=-1, keepdims=True)
        y = _silu(d * lax.rsqrt(var + LN_EPS) * lng_ref[...] + lnb_ref[...])
        y = _dot1(y, pw_ref[...]) + pwb_ref[...]
        o_ref[r0:r0 + CONV_SUB, :] = y * _silu(pc_ref[r0:r0 + CONV_SUB, 2 * CONV_WIDTH:3 * CONV_WIDTH])
    u_ref[0:CONV_HALO, :] = u_ref[tt:tt + CONV_HALO, :]


def _conv(pc, dw, dwb, lng, lnb, pw, pwb, bsz, tlen, tt):
    nt = tlen // tt
    full = lambda a: pl.BlockSpec(a.shape, lambda b, t: (0, 0))
    return pl.pallas_call(
        _conv_kernel,
        grid=(bsz, nt),
        in_specs=[pl.BlockSpec((tt, 3 * CONV_WIDTH), lambda b, t: (b * nt + t, 0)),
                  full(dw), full(dwb), full(lng), full(lnb), full(pw), full(pwb)],
        out_specs=pl.BlockSpec((tt, CONV_WIDTH), lambda b, t: (b * nt + t, 0)),
        out_shape=jax.ShapeDtypeStruct((bsz * tlen, CONV_WIDTH), F32),
        scratch_shapes=[pltpu.VMEM((CONV_HALO + tt, CONV_WIDTH), F32)],
        compiler_params=pltpu.CompilerParams(dimension_semantics=("parallel", "arbitrary"),
                                             vmem_limit_bytes=VMEM_LIMIT),
        name="conv",
    )(pc, dw, dwb, lng, lnb, pw, pwb)


def _sb_kernel(q_ref, k_ref, v_ref, g_ref, o_ref):
    i = pl.program_id(2)
    bq = BLOCK_Q
    lane = lax.broadcasted_iota(jnp.int32, (bq, LANES), 1)
    head0 = lane < HEAD_DIM
    q = q_ref[...]
    qs = jnp.concatenate([jnp.where(head0, q, 0.0), jnp.where(head0, 0.0, q)], axis=0).astype(BF16)
    row = lax.broadcasted_iota(jnp.int32, (LANES, 2 * LANES), 0)
    col = lax.broadcasted_iota(jnp.int32, (LANES, 2 * LANES), 1)
    cs = jnp.where((col >= LANES) | (row > col), 1.0, 0.0).astype(BF16)
    scale = HEAD_DIM ** -0.5

    def block(j, carry, acc, diagonal):
        start = pl.multiple_of(j * bq, bq)
        kb = k_ref[pl.ds(start, bq), :]
        vb = v_ref[pl.ds(start, bq), :]
        z = _dot_nt1(qs, kb) * scale
        t = jnp.log1p(jnp.exp(-jnp.abs(z)))
        log_keep = -jnp.maximum(z, 0.0) - t
        log_beta = jnp.minimum(z, 0.0) - t
        if diagonal:
            qi = lax.broadcasted_iota(jnp.int32, (2 * bq, bq), 0) % bq
            ki = lax.broadcasted_iota(jnp.int32, (2 * bq, bq), 1)
            causal = ki < qi
            log_keep = jnp.where(causal, log_keep, 0.0)
        sums = _dot2_exact_rhs(log_keep, cs)
        log_rest = sums[:, :LANES] + carry
        att = jnp.exp(log_beta + log_rest)
        if diagonal:
            att = jnp.where(causal, att, 0.0)
        acc = acc + _dot1(att, vb)
        return carry + sums[:, LANES:], acc

    zeros = jnp.zeros((2 * bq, LANES), F32)
    carry, acc = block(i, zeros, zeros, True)

    def body(jj, c):
        return block(i - 1 - jj, c[0], c[1], False)

    carry, acc = lax.fori_loop(0, i, body, (carry, acc))
    o = jnp.where(head0, acc[:bq], acc[bq:])
    o_ref[...] = o * _silu(g_ref[...])


def _sb(ps, bsz, tlen):
    nq = tlen // BLOCK_Q
    return pl.pallas_call(
        _sb_kernel,
        grid=(bsz, PAIRS, nq),
        in_specs=[pl.BlockSpec((BLOCK_Q, LANES), lambda b, p, i: (b * nq + i, p)),
                  pl.BlockSpec((tlen, LANES), lambda b, p, i: (b, PAIRS + p)),
                  pl.BlockSpec((tlen, LANES), lambda b, p, i: (b, 2 * PAIRS + p)),
                  pl.BlockSpec((BLOCK_Q, LANES), lambda b, p, i: (b * nq + i, 3 * PAIRS + p))],
        out_specs=pl.BlockSpec((BLOCK_Q, LANES), lambda b, p, i: (b * nq + i, p)),
        out_shape=jax.ShapeDtypeStruct((bsz * tlen, SB_WIDTH), F32),
        compiler_params=pltpu.CompilerParams(dimension_semantics=("parallel", "parallel", "arbitrary"),
                                             vmem_limit_bytes=VMEM_LIMIT),
        name="stickbreak",
    )(ps, ps, ps, ps)


def _stack_heads(a, head0):
    return jnp.concatenate([jnp.where(head0, a, 0.0), jnp.where(head0, 0.0, a)], axis=0)


def _rwkv_kernel(*refs, has_vres):
    if has_vres:
        (pr_ref, vf_ref, mu_ref, w0_ref, w2_ref, a0_ref, a2_ref, kks_ref, ka_ref, rk_ref, gng_ref, gnb_ref,
         v0_ref, v1_ref, v2_ref, y_ref, xs_ref, st_ref) = refs
    else:
        (pr_ref, mu_ref, w0_ref, w2_ref, a0_ref, a2_ref, kks_ref, ka_ref, rk_ref, gng_ref, gnb_ref,
         y_ref, vf_out_ref, xs_ref, st_ref) = refs
    c = CHUNK
    w = RWKV_WIDTH

    @pl.when(pl.program_id(1) == 0)
    def _():
        xs_ref[0:8, :] = jnp.zeros((8, SHIFT_COLS), F32)
        st_ref[...] = jnp.zeros(st_ref.shape, F32)

    x = pr_ref[:, 0:SHIFT_COLS]
    xs_ref[8:8 + c, :] = x
    prev = xs_ref[7:7 + c, :]
    xs_ref[7:8, :] = x[c - 1:c, :]
    f = x + (prev - x) * mu_ref[...]
    r = f[:, 0:w]
    k = f[:, w:2 * w]
    v = f[:, 2 * w:3 * w]
    wa = f[:, 3 * w:3 * w + WA_WIDTH]

    w_pre = w0_ref[...] + _dot3(jnp.tanh(wa), w2_ref[...])
    logw = -jnp.exp(-_softplus(-w_pre) - 0.5)
    a_lr = _sigmoid(a0_ref[...] + _dot3(wa, a2_ref[...]))
    if has_vres:
        gate = _sigmoid(v0_ref[...] + _dot3(_dot3(v, v1_ref[...]), v2_ref[...]))
        v = v + (vf_ref[...] - v) * gate
    else:
        vf_out_ref[...] = v

    lane = lax.broadcasted_iota(jnp.int32, (c, LANES), 1)
    head0 = lane < HEAD_DIM
    ri = lax.broadcasted_iota(jnp.int32, (LANES, LANES), 0)
    ci = lax.broadcasted_iota(jnp.int32, (LANES, LANES), 1)
    head_ones = jnp.where((ri < HEAD_DIM) == (ci < HEAD_DIM), 1.0, 0.0).astype(BF16)
    lower = ri > ci
    lower_eq = ri >= ci
    eye = jnp.where(ri == ci, 1.0, 0.0)
    rc = lax.broadcasted_iota(jnp.int32, (c, c), 0)
    cc = lax.broadcasted_iota(jnp.int32, (c, c), 1)
    cum = jnp.where(rc >= cc, 1.0, 0.0).astype(BF16)

    logp = jnp.dot(cum, _split(logw)[0], preferred_element_type=F32) + \
        jnp.dot(cum, _split(logw)[1], preferred_element_type=F32)
    p = jnp.exp(logp)
    p_prev = jnp.exp(logp - logw)
    inv_p = jnp.exp(-logp)
    p_end = p[c - 1:c, :]

    kk = k * kks_ref[...]
    k2 = k * (1.0 + (a_lr - 1.0) * ka_ref[...])
    rkk = r * k2 * rk_ref[...]

    for pr in range(PAIRS):
        sl = slice(pr * LANES, (pr + 1) * LANES)
        kk_p = kk[:, sl]
        n2 = _dot2_exact_rhs(kk_p * kk_p, head_ones)
        kk_p = kk_p / jnp.maximum(jnp.sqrt(n2), 1e-12)
        a_t = _stack_heads(-kk_p * p_prev[:, sl], head0)
        b_t = _stack_heads(kk_p * a_lr[:, sl] * inv_p[:, sl], head0)
        k_t = _stack_heads(k2[:, sl] * inv_p[:, sl], head0)
        r_t = _stack_heads(r[:, sl] * p[:, sl], head0)
        v_s = _stack_heads(v[:, sl], head0)
        pe = p_end[:, sl]
        b_e = b_t * pe
        k_e = k_t * pe

        ab = jnp.where(lower, _dot_nt(a_t, b_t), 0.0)
        ak = jnp.where(lower, _dot_nt(a_t, k_t), 0.0)
        rb = jnp.where(lower_eq, _dot_nt(r_t, b_t), 0.0)
        rkm = jnp.where(lower_eq, _dot_nt(r_t, k_t), 0.0)

        tinv = eye + ab
        pw = ab
        for _ in range(5):
            pw = _dotm(pw, pw)
            tinv = tinv + _dotm(tinv, pw)

        g = st_ref[pr]
        wmat = _dotm(tinv, a_t)
        u_loc = _dotm(tinv, _dotm(ak, v_s))
        u = _dotm(wmat, g) + u_loc
        y_s = _dotm(r_t, g) + _dotm(rb, u) + _dotm(rkm, v_s)
        pcol = jnp.broadcast_to(pe, (LANES, LANES)).T
        st_ref[pr] = pcol * g + _dotm(b_e.T, u) + _dotm(k_e.T, v_s)

        y = y_s[:c] + y_s[c:]
        mean = _dot2_exact_rhs(y, head_ones) * (1.0 / HEAD_DIM)
        d = y - mean
        var = _dot2_exact_rhs(d * d, head_ones) * (1.0 / HEAD_DIM)
        y = d * lax.rsqrt(var + GN_EPS) * gng_ref[:, sl] + gnb_ref[:, sl]
        y = y + _dot2_exact_rhs(rkk[:, sl], head_ones) * v[:, sl]
        y_ref[:, sl] = y * _silu(pr_ref[:, SHIFT_COLS + pr * LANES:SHIFT_COLS + (pr + 1) * LANES])


def _dot_nt(a, b):
    ah, al = _split(a)
    bh, bl = _split(b)
    d = functools.partial(lax.dot_general, dimension_numbers=(((1,), (1,)), ((), ())),
                          preferred_element_type=F32)
    return d(ah, bh) + (d(ah, bl) + d(al, bh))


def _dotm(a, b):
    return _dot3(a, b)


def _rwkv(pr, vfirst, mu, w0, w2, a0, a2, kks, ka, rk, gng, gnb, vres, bsz, tlen):
    nc = tlen // CHUNK
    has_vres = vres is not None
    full = lambda a: pl.BlockSpec(a.shape, lambda b, t: (0, 0))
    row_spec = lambda width: pl.BlockSpec((CHUNK, width), lambda b, t: (b * nc + t, 0))
    params = [mu, w0, w2, a0, a2, kks, ka, rk, gng, gnb]
    in_specs = [row_spec(R_IN)]
    args = [pr]
    if has_vres:
        in_specs.append(row_spec(RWKV_WIDTH))
        args.append(vfirst)
        params = params + list(vres)
    in_specs += [full(a) for a in params]
    args += params
    y_shape = jax.ShapeDtypeStruct((bsz * tlen, RWKV_WIDTH), F32)
    if has_vres:
        out_specs, out_shape = row_spec(RWKV_WIDTH), y_shape
    else:
        out_specs, out_shape = [row_spec(RWKV_WIDTH), row_spec(RWKV_WIDTH)], [y_shape, y_shape]
    return pl.pallas_call(
        functools.partial(_rwkv_kernel, has_vres=has_vres),
        grid=(bsz, nc),
        in_specs=in_specs,
        out_specs=out_specs,
        out_shape=out_shape,
        scratch_shapes=[pltpu.VMEM((8 + CHUNK, SHIFT_COLS), F32),
                        pltpu.VMEM((PAIRS, LANES, LANES), F32)],
        compiler_params=pltpu.CompilerParams(dimension_semantics=("parallel", "arbitrary"),
                                             vmem_limit_bytes=VMEM_LIMIT),
        name="rwkv7_vres" if has_vres else "rwkv7",
    )(*args)


def _outproj_kernel(x_ref, yc_ref, ys_ref, yr_ref, wc_ref, ws_ref, wr_ref, g_ref, o_ref):
    m = _dot1(yc_ref[...], wc_ref[...]) + _dot1(ys_ref[...], ws_ref[...]) + _dot1(yr_ref[...], wr_ref[...])
    ms = jnp.mean(m * m, axis=-1, keepdims=True)
    o_ref[...] = x_ref[...] + m * lax.rsqrt(ms + RMS_EPS) * g_ref[...]


def _outproj(x2, yc, ys, yr, wc, ws, wr, g, tm):
    n = x2.shape[0]
    full = lambda a: pl.BlockSpec(a.shape, lambda i: (0, 0))
    rows = lambda a: pl.BlockSpec((tm, a.shape[1]), lambda i: (i, 0))
    return pl.pallas_call(
        _outproj_kernel,
        grid=(n // tm,),
        in_specs=[rows(x2), rows(yc), rows(ys), rows(yr), full(wc), full(ws), full(wr), full(g)],
        out_specs=rows(x2),
        out_shape=jax.ShapeDtypeStruct(x2.shape, F32),
        compiler_params=pltpu.CompilerParams(dimension_semantics=("parallel",),
                                             vmem_limit_bytes=VMEM_LIMIT),
        name="outproj",
    )(x2, yc, ys, yr, wc, ws, wr, g)


def _pad_rows(a, rows):
    return jnp.concatenate([a, jnp.zeros((rows - a.shape[0],) + a.shape[1:], a.dtype)], axis=0)


def kernel(x, pre_norm_g, post_norm_g, w_in, w_out, conv_dw, conv_dw_b, conv_ln_g, conv_ln_b, conv_pw, conv_pw_b, rwkv_mu, rwkv_w0, rwkv_w2, rwkv_a0, rwkv_a2, rwkv_kk_scale, rwkv_ka, rwkv_rk, rwkv_gn_g, rwkv_gn_b, rwkv_v0, rwkv_v1, rwkv_v2):
    bsz, tlen, dm = x.shape
    depth = w_in.shape[0]
    assert dm == D_MODEL and tlen % BLOCK_Q == 0 and tlen % CHUNK == 0
    n = bsz * tlen
    tm = 256 if n % 256 == 0 else BLOCK_Q
    tt = 256 if tlen % 256 == 0 else BLOCK_Q
    row = lambda a: a.reshape(1, -1)
    c_end = 3 * CONV_WIDTH
    s_end = c_end + 4 * SB_WIDTH
    r_end = s_end + 3 * RWKV_WIDTH
    x2 = x.reshape(n, dm)
    vfirst = None
    for l in range(depth):
        wi = w_in[l]
        wa_cols = jnp.concatenate([wi[:, r_end:r_end + 2 * LORA],
                                   jnp.zeros((dm, WA_WIDTH - 2 * LORA), wi.dtype)], axis=1)
        w_r = jnp.concatenate([wi[:, s_end:r_end], wa_cols, wi[:, r_end + 2 * LORA:]], axis=1).astype(BF16)
        pc, ps, pr = _inproj(x2, row(pre_norm_g[l]), wi[:, :c_end].astype(BF16),
                             wi[:, c_end:s_end].astype(BF16), w_r, tm)

        yc = _conv(pc, conv_dw[l], row(conv_dw_b[l]), row(conv_ln_g[l]), row(conv_ln_b[l]),
                   conv_pw[l].astype(BF16), row(conv_pw_b[l]), bsz, tlen, tt)
        ys = _sb(ps, bsz, tlen)

        mu = jnp.concatenate([rwkv_mu[l], jnp.zeros((WA_WIDTH - 2 * LORA,), F32)]).reshape(1, -1)
        w2 = _pad_rows(rwkv_w2[l], WA_WIDTH)
        a2 = _pad_rows(jnp.concatenate([jnp.zeros((LORA, RWKV_WIDTH), F32), rwkv_a2[l]], axis=0), WA_WIDTH)
        vres = None
        if l > 0:
            v1 = jnp.concatenate([rwkv_v1[l - 1], jnp.zeros((RWKV_WIDTH, LANES - LORA), F32)], axis=1)
            vres = (row(rwkv_v0[l - 1]), v1, _pad_rows(rwkv_v2[l - 1], LANES))
        res = _rwkv(pr, vfirst, mu, row(rwkv_w0[l]), w2, row(rwkv_a0[l]), a2, row(rwkv_kk_scale[l]),
                    row(rwkv_ka[l]), row(rwkv_rk[l]), row(rwkv_gn_g[l]), row(rwkv_gn_b[l]), vres, bsz, tlen)
        if l == 0:
            yr, vfirst = res
        else:
            yr = res

        wo = w_out[l].astype(BF16)
        x2 = _outproj(x2, yc, ys, yr, wo[:CONV_WIDTH], wo[CONV_WIDTH:CONV_WIDTH + SB_WIDTH],
                      wo[CONV_WIDTH + SB_WIDTH:], row(post_norm_g[l]), tm)
    return x2.reshape(bsz, tlen, dm)
```

```python
import functools

import jax
import jax.numpy as jnp
from jax import lax
from jax.experimental import pallas as pl
from jax.experimental.pallas import tpu as pltpu

D_MODEL = 1024
HEAD_DIM = 64
CONV_WIDTH = 256
SB_WIDTH = 384
RWKV_WIDTH = 384
CONV_TAPS = 31
LORA = 32
BLOCK_Q = 128
RMS_EPS = 1e-6
LN_EPS = 1e-5
GN_EPS = 64e-5

LANES = 128
PAIRS = RWKV_WIDTH // LANES
CHUNK = 64
RWKV_SEQS = 2
WA_WIDTH = LANES
R_IN = 3 * RWKV_WIDTH + WA_WIDTH + RWKV_WIDTH
SHIFT_COLS = 3 * RWKV_WIDTH + WA_WIDTH
VMEM_LIMIT = 56 * 1024 * 1024

F32 = jnp.float32
BF16 = jnp.bfloat16


def _dot1(a, b):
    return jnp.dot(a.astype(BF16), b.astype(BF16), preferred_element_type=F32)


def _split(a):
    hi = a.astype(BF16)
    lo = (a - hi.astype(F32)).astype(BF16)
    return hi, lo


def _dot3(a, b):
    ah, al = _split(a)
    bh, bl = _split(b)
    d = functools.partial(jnp.dot, preferred_element_type=F32)
    return d(ah, bh) + (d(ah, bl) + d(al, bh))


def _dot2_exact_rhs(a, b_bf16):
    ah, al = _split(a)
    d = functools.partial(jnp.dot, preferred_element_type=F32)
    return d(ah, b_bf16) + d(al, b_bf16)


def _dot_nt1(a, b):
    return lax.dot_general(a.astype(BF16), b.astype(BF16), (((1,), (1,)), ((), ())),
                           preferred_element_type=F32)


def _sigmoid(x):
    return 1.0 / (1.0 + jnp.exp(-x))


def _silu(x):
    return x * _sigmoid(x)


def _softplus(x):
    return jnp.maximum(x, 0.0) + jnp.log(1.0 + jnp.exp(-jnp.abs(x)))


def _inproj_kernel(x_ref, g_ref, wc_ref, ws_ref, wr_ref, oc_ref, os_ref, or_ref):
    x = x_ref[...]
    ms = jnp.mean(x * x, axis=-1, keepdims=True)
    h = (x * lax.rsqrt(ms + RMS_EPS) * g_ref[...]).astype(BF16)
    for w_ref, o_ref in ((wc_ref, oc_ref), (ws_ref, os_ref), (wr_ref, or_ref)):
        n = o_ref.shape[1]
        for c0 in range(0, n, 256):
            c1 = min(c0 + 256, n)
            o_ref[:, c0:c1] = jnp.dot(h, w_ref[:, c0:c1], preferred_element_type=F32)


def _inproj(x2, g, wc, ws, wr, tm):
    n = x2.shape[0]
    full = lambda a: pl.BlockSpec(a.shape, lambda i: (0, 0))
    return pl.pallas_call(
        _inproj_kernel,
        grid=(n // tm,),
        in_specs=[pl.BlockSpec((tm, D_MODEL), lambda i: (i, 0)), full(g), full(wc), full(ws), full(wr)],
        out_specs=[pl.BlockSpec((tm, wc.shape[1]), lambda i: (i, 0)),
                   pl.BlockSpec((tm, ws.shape[1]), lambda i: (i, 0)),
                   pl.BlockSpec((tm, wr.shape[1]), lambda i: (i, 0))],
        out_shape=[jax.ShapeDtypeStruct((n, wc.shape[1]), F32),
                   jax.ShapeDtypeStruct((n, ws.shape[1]), F32),
                   jax.ShapeDtypeStruct((n, wr.shape[1]), F32)],
        compiler_params=pltpu.CompilerParams(dimension_semantics=("parallel",),
                                             vmem_limit_bytes=VMEM_LIMIT),
        name="inproj",
    )(x2, g, wc, ws, wr)


CONV_HALO = 32
CONV_SUB = 64


def _conv_kernel(pc_ref, dw_ref, dwb_ref, lng_ref, lnb_ref, pw_ref, pwb_ref, o_ref, u_ref):
    tt = o_ref.shape[0]

    @pl.when(pl.program_id(1) == 0)
    def _():
        u_ref[0:CONV_HALO, :] = jnp.zeros((CONV_HALO, CONV_WIDTH), F32)

    u_ref[CONV_HALO:CONV_HALO + tt, :] = pc_ref[:, 0:CONV_WIDTH] * _sigmoid(pc_ref[:, CONV_WIDTH:2 * CONV_WIDTH])
    for r0 in range(0, tt, CONV_SUB):
        acc = jnp.broadcast_to(dwb_ref[...], (CONV_SUB, CONV_WIDTH))
        for tap in range(CONV_TAPS):
            s = CONV_HALO + r0 - (CONV_TAPS - 1) + tap
            acc = acc + dw_ref[tap:tap + 1, :] * u_ref[s:s + CONV_SUB, :]
        mean = jnp.mean(acc, axis=-1, keepdims=True)
        d = acc - mean
        var = jnp.mean(d * d, axis=-1, keepdims=True)
        y = _silu(d * lax.rsqrt(var + LN_EPS) * lng_ref[...] + lnb_ref[...])
        y = _dot1(y, pw_ref[...]) + pwb_ref[...]
        o_ref[r0:r0 + CONV_SUB, :] = y * _silu(pc_ref[r0:r0 + CONV_SUB, 2 * CONV_WIDTH:3 * CONV_WIDTH])
    u_ref[0:CONV_HALO, :] = u_ref[tt:tt + CONV_HALO, :]


def _conv(pc, dw, dwb, lng, lnb, pw, pwb, bsz, tlen, tt):
    nt = tlen // tt
    full = lambda a: pl.BlockSpec(a.shape, lambda b, t: (0, 0))
    return pl.pallas_call(
        _conv_kernel,
        grid=(bsz, nt),
        in_specs=[pl.BlockSpec((tt, 3 * CONV_WIDTH), lambda b, t: (b * nt + t, 0)),
                  full(dw), full(dwb), full(lng), full(lnb), full(pw), full(pwb)],
        out_specs=pl.BlockSpec((tt, CONV_WIDTH), lambda b, t: (b * nt + t, 0)),
        out_shape=jax.ShapeDtypeStruct((bsz * tlen, CONV_WIDTH), F32),
        scratch_shapes=[pltpu.VMEM((CONV_HALO + tt, CONV_WIDTH), F32)],
        compiler_params=pltpu.CompilerParams(dimension_semantics=("parallel", "arbitrary"),
                                             vmem_limit_bytes=VMEM_LIMIT),
        name="conv",
    )(pc, dw, dwb, lng, lnb, pw, pwb)


SB_DEAD_LOG = -150.0


def _sb_kernel(q_ref, k_ref, v_ref, g_ref, o_ref, carry_ref, acc_ref):
    i = pl.program_id(1)
    bq = BLOCK_Q
    lane = lax.broadcasted_iota(jnp.int32, (bq, LANES), 1)
    head0 = lane < HEAD_DIM
    row = lax.broadcasted_iota(jnp.int32, (LANES, 2 * LANES), 0)
    col = lax.broadcasted_iota(jnp.int32, (LANES, 2 * LANES), 1)
    cs = jnp.where((col >= LANES) | (row > col), 1.0, 0.0).astype(BF16)
    scale = HEAD_DIM ** -0.5

    def block(j, diagonal):
        start = pl.multiple_of(j * bq, bq)
        pairs = range(PAIRS)
        sls = [slice(p * LANES, (p + 1) * LANES) for p in pairs]
        zs = []
        for p in pairs:
            q = q_ref[:, sls[p]]
            qs = jnp.concatenate([jnp.where(head0, q, 0.0), jnp.where(head0, 0.0, q)], axis=0).astype(BF16)
            zs.append(_dot_nt1(qs, k_ref[pl.ds(start, bq), sls[p]]))
        if diagonal:
            qi = lax.broadcasted_iota(jnp.int32, (2 * bq, bq), 0) % bq
            ki = lax.broadcasted_iota(jnp.int32, (2 * bq, bq), 1)
            causal = ki < qi
        log_betas, sums = [], []
        for p in pairs:
            z = zs[p] * scale
            t = jnp.log(1.0 + jnp.exp(-jnp.abs(z)))
            log_keep = -jnp.maximum(z, 0.0) - t
            log_betas.append(jnp.minimum(z, 0.0) - t)
            if diagonal:
                log_keep = jnp.where(causal, log_keep, 0.0)
            sums.append(_dot2_exact_rhs(log_keep, cs))
        live = None
        pvs = []
        for p in pairs:
            if diagonal:
                att = jnp.where(causal, jnp.exp(log_betas[p] + sums[p][:, :LANES]), 0.0)
                carry = sums[p][:, LANES:]
            else:
                att = jnp.exp(log_betas[p] + (sums[p][:, :LANES] + carry_ref[p]))
                carry = carry_ref[p] + sums[p][:, LANES:]
            carry_ref[p] = carry
            m = jnp.max(carry)
            live = m if live is None else jnp.maximum(live, m)
            pvs.append(_dot1(att, v_ref[pl.ds(start, bq), sls[p]]))
        for p in pairs:
            if diagonal:
                acc_ref[p] = pvs[p]
            else:
                acc_ref[p] += pvs[p]
        return live

    block(i, True)

    def cond(c):
        return (c[0] < i) & (c[1] > SB_DEAD_LOG)

    def body(c):
        return c[0] + 1, block(i - 1 - c[0], False)

    lax.while_loop(cond, body, (jnp.int32(0), jnp.float32(0.0)))
    for p in range(PAIRS):
        sl = slice(p * LANES, (p + 1) * LANES)
        o = jnp.where(head0, acc_ref[p, :bq], acc_ref[p, bq:])
        o_ref[:, sl] = o * _silu(g_ref[:, sl])


def _sb(ps, bsz, tlen):
    nq = tlen // BLOCK_Q
    return pl.pallas_call(
        _sb_kernel,
        grid=(bsz, nq),
        in_specs=[pl.BlockSpec((BLOCK_Q, SB_WIDTH), lambda b, i: (b * nq + i, 0)),
                  pl.BlockSpec((tlen, SB_WIDTH), lambda b, i: (b, 1)),
                  pl.BlockSpec((tlen, SB_WIDTH), lambda b, i: (b, 2)),
                  pl.BlockSpec((BLOCK_Q, SB_WIDTH), lambda b, i: (b * nq + i, 3))],
        out_specs=pl.BlockSpec((BLOCK_Q, SB_WIDTH), lambda b, i: (b * nq + i, 0)),
        out_shape=jax.ShapeDtypeStruct((bsz * tlen, SB_WIDTH), F32),
        scratch_shapes=[pltpu.VMEM((PAIRS, 2 * BLOCK_Q, LANES), F32),
                        pltpu.VMEM((PAIRS, 2 * BLOCK_Q, LANES), F32)],
        compiler_params=pltpu.CompilerParams(dimension_semantics=("parallel", "arbitrary"),
                                             vmem_limit_bytes=VMEM_LIMIT),
        name="stickbreak",
    )(ps, ps, ps, ps)


def _stack_heads(a, head0):
    return jnp.concatenate([jnp.where(head0, a, 0.0), jnp.where(head0, 0.0, a)], axis=0)


def _rwkv_kernel(*refs, has_vres):
    if has_vres:
        (pr_ref, vf_ref, mu_ref, w0_ref, w2_ref, a0_ref, a2_ref, kks_ref, ka_ref, rk_ref, gng_ref, gnb_ref,
         v0_ref, v1_ref, v2_ref, y_ref, xs_ref, st_ref) = refs
    else:
        (pr_ref, mu_ref, w0_ref, w2_ref, a0_ref, a2_ref, kks_ref, ka_ref, rk_ref, gng_ref, gnb_ref,
         y_ref, vf_out_ref, xs_ref, st_ref) = refs
    c = CHUNK
    w = RWKV_WIDTH
    nb = pr_ref.shape[0]

    @pl.when(pl.program_id(1) == 0)
    def _():
        xs_ref[:, 0:8, :] = jnp.zeros((nb, 8, SHIFT_COLS), F32)
        st_ref[...] = jnp.zeros(st_ref.shape, F32)

    lane = lax.broadcasted_iota(jnp.int32, (c, LANES), 1)
    head0 = lane < HEAD_DIM
    ri = lax.broadcasted_iota(jnp.int32, (LANES, LANES), 0)
    ci = lax.broadcasted_iota(jnp.int32, (LANES, LANES), 1)
    head_ones = jnp.where((ri < HEAD_DIM) == (ci < HEAD_DIM), 1.0, 0.0).astype(BF16)
    lower = ri > ci
    lower_eq = ri >= ci
    eye = jnp.where(ri == ci, 1.0, 0.0)
    rc = lax.broadcasted_iota(jnp.int32, (c, c), 0)
    cc = lax.broadcasted_iota(jnp.int32, (c, c), 1)
    cum = jnp.where(rc >= cc, 1.0, 0.0).astype(BF16)

    seqs = []
    for b in range(nb):
        x = pr_ref[b, :, 0:SHIFT_COLS]
        xs_ref[b, 8:8 + c, :] = x
        prev = xs_ref[b, 7:7 + c, :]
        xs_ref[b, 7:8, :] = x[c - 1:c, :]
        f = x + (prev - x) * mu_ref[...]
        r = f[:, 0:w]
        k = f[:, w:2 * w]
        v = f[:, 2 * w:3 * w]
        wa = f[:, 3 * w:3 * w + WA_WIDTH]
        w_pre = w0_ref[...] + _dot3(jnp.tanh(wa), w2_ref[...])
        logw = -jnp.exp(-_softplus(-w_pre) - 0.5)
        a_lr = _sigmoid(a0_ref[...] + _dot3(wa, a2_ref[...]))
        if has_vres:
            gate = _sigmoid(v0_ref[...] + _dot3(_dot3(v, v1_ref[...]), v2_ref[...]))
            v = v + (vf_ref[b] - v) * gate
        else:
            vf_out_ref[b] = v
        lw_hi, lw_lo = _split(logw)
        logp = _mm(cum, lw_hi) + _mm(cum, lw_lo)
        seqs.append(dict(r=r, v=v, a_lr=a_lr, p=jnp.exp(logp), p_prev=jnp.exp(logp - logw),
                         inv_p=jnp.exp(-logp), kk=k * kks_ref[...],
                         k2=k * (1.0 + (a_lr - 1.0) * ka_ref[...])))

    chains = []
    for b, s in enumerate(seqs):
        for pr in range(PAIRS):
            sl = slice(pr * LANES, (pr + 1) * LANES)
            kk_p = s["kk"][:, sl]
            n2 = _dot2_exact_rhs(kk_p * kk_p, head_ones)
            kk_p = kk_p / jnp.maximum(jnp.sqrt(n2), 1e-12)
            a_t = _stack_heads(-kk_p * s["p_prev"][:, sl], head0).astype(BF16)
            b_f = _stack_heads(kk_p * s["a_lr"][:, sl] * s["inv_p"][:, sl], head0)
            k_f = _stack_heads(s["k2"][:, sl] * s["inv_p"][:, sl], head0)
            r_t = _stack_heads(s["r"][:, sl] * s["p"][:, sl], head0).astype(BF16)
            pe = s["p"][c - 1:c, sl]
            ch = dict(b=b, sl=sl, idx=b * PAIRS + pr, a_t=a_t, r_t=r_t, pe=pe,
                      v_s=_stack_heads(s["v"][:, sl], head0).astype(BF16),
                      be_ke=jnp.concatenate([(b_f * pe).T, (k_f * pe).T], axis=1).astype(BF16),
                      rkk=s["r"][:, sl] * s["k2"][:, sl] * rk_ref[:, sl], v=s["v"][:, sl])
            ch["gram"] = _mm_nt(jnp.concatenate([a_t, r_t], axis=0),
                                jnp.concatenate([b_f, k_f], axis=0).astype(BF16))
            chains.append(ch)

    for ch in chains:
        gram = ch.pop("gram")
        ab = jnp.where(lower, gram[:LANES, :LANES], 0.0)
        ak = jnp.where(lower, gram[:LANES, LANES:], 0.0).astype(BF16)
        ch["rb_rk"] = jnp.concatenate([jnp.where(lower_eq, gram[LANES:, :LANES], 0.0),
                                       jnp.where(lower_eq, gram[LANES:, LANES:], 0.0)], axis=1).astype(BF16)
        ch["akv"] = _mm(ak, ch["v_s"])
        g = st_ref[ch["idx"]]
        ch["g"] = g
        ch["ar_g"] = _mm(jnp.concatenate([ch["a_t"], ch["r_t"]], axis=0), g.astype(BF16))
        ch["s"] = eye + ab
        ch["pw"] = ab.astype(BF16)
    for ch in chains:
        ch["pw"] = _mm(ch["pw"], ch["pw"]).astype(BF16)
    for _ in range(4):
        for ch in chains:
            res = _mm(ch["pw"], jnp.concatenate([ch["s"].astype(BF16), ch["pw"]], axis=1))
            ch["s"] = ch["s"] + res[:, :LANES]
            ch["pw"] = res[:, LANES:].astype(BF16)
    for ch in chains:
        ch["s"] = (ch["s"] + _mm(ch["pw"], ch["s"].astype(BF16))).astype(BF16)
    for ch in chains:
        x = ch["ar_g"][:LANES] + ch["akv"]
        ch["u"] = _mm(ch["s"], x.astype(BF16)).astype(BF16)
    for ch in chains:
        lhs = jnp.concatenate([ch["be_ke"], ch["rb_rk"]], axis=0)
        ch["fin"] = _mm(lhs, jnp.concatenate([ch["u"], ch["v_s"]], axis=0))
    for ch in chains:
        pcol = jnp.broadcast_to(ch["pe"], (LANES, LANES)).T
        st_ref[ch["idx"]] = pcol * ch["g"] + ch["fin"][:LANES]
        y_s = ch["ar_g"][LANES:] + ch["fin"][LANES:]
        ch["y"] = y_s[:c] + y_s[c:]
        ch["mean"] = _dot2_exact_rhs(ch["y"], head_ones) * (1.0 / HEAD_DIM)
    for ch in chains:
        d = ch["y"] - ch["mean"]
        ch["d"] = d
        ch["var"] = _dot2_exact_rhs(d * d, head_ones) * (1.0 / HEAD_DIM)
        ch["bonus"] = _dot2_exact_rhs(ch["rkk"], head_ones)
    for ch in chains:
        b, sl = ch["b"], ch["sl"]
        y = ch["d"] * lax.rsqrt(ch["var"] + GN_EPS) * gng_ref[:, sl] + gnb_ref[:, sl]
        y = y + ch["bonus"] * ch["v"]
        y_ref[b, :, sl] = y * _silu(pr_ref[b, :, SHIFT_COLS + sl.start:SHIFT_COLS + sl.stop])


def _mm(a, b):
    return jnp.dot(a, b, preferred_element_type=F32)


def _mm_nt(a, b):
    return lax.dot_general(a, b, (((1,), (1,)), ((), ())), preferred_element_type=F32)


def _rwkv(pr, vfirst, mu, w0, w2, a0, a2, kks, ka, rk, gng, gnb, vres, bsz, tlen):
    nc = tlen // CHUNK
    nb = RWKV_SEQS if bsz % RWKV_SEQS == 0 else 1
    has_vres = vres is not None
    full = lambda a: pl.BlockSpec(a.shape, lambda b, t: (0, 0))
    row_spec = lambda width: pl.BlockSpec((nb, CHUNK, width), lambda b, t: (b, t, 0))
    params = [mu, w0, w2, a0, a2, kks, ka, rk, gng, gnb]
    in_specs = [row_spec(R_IN)]
    args = [pr]
    if has_vres:
        in_specs.append(row_spec(RWKV_WIDTH))
        args.append(vfirst)
        params = params + list(vres)
    in_specs += [full(a) for a in params]
    args += params
    y_shape = jax.ShapeDtypeStruct((bsz, tlen, RWKV_WIDTH), F32)
    if has_vres:
        out_specs, out_shape = row_spec(RWKV_WIDTH), y_shape
    else:
        out_specs, out_shape = [row_spec(RWKV_WIDTH), row_spec(RWKV_WIDTH)], [y_shape, y_shape]
    return pl.pallas_call(
        functools.partial(_rwkv_kernel, has_vres=has_vres),
        grid=(bsz // nb, nc),
        in_specs=in_specs,
        out_specs=out_specs,
        out_shape=out_shape,
        scratch_shapes=[pltpu.VMEM((nb, 8 + CHUNK, SHIFT_COLS), F32),
                        pltpu.VMEM((nb * PAIRS, LANES, LANES), F32)],
        compiler_params=pltpu.CompilerParams(dimension_semantics=("parallel", "arbitrary"),
                                             vmem_limit_bytes=VMEM_LIMIT),
        name="rwkv7_vres" if has_vres else "rwkv7",
    )(*args)


def _outproj_kernel(x_ref, yc_ref, ys_ref, yr_ref, wc_ref, ws_ref, wr_ref, g_ref, o_ref):
    m = _dot1(yc_ref[...], wc_ref[...]) + _dot1(ys_ref[...], ws_ref[...]) + _dot1(yr_ref[...], wr_ref[...])
    ms = jnp.mean(m * m, axis=-1, keepdims=True)
    o_ref[...] = x_ref[...] + m * lax.rsqrt(ms + RMS_EPS) * g_ref[...]


def _outproj(x2, yc, ys, yr, wc, ws, wr, g, tm):
    n = x2.shape[0]
    full = lambda a: pl.BlockSpec(a.shape, lambda i: (0, 0))
    rows = lambda a: pl.BlockSpec((tm, a.shape[1]), lambda i: (i, 0))
    return pl.pallas_call(
        _outproj_kernel,
        grid=(n // tm,),
        in_specs=[rows(x2), rows(yc), rows(ys), rows(yr), full(wc), full(ws), full(wr), full(g)],
        out_specs=rows(x2),
        out_shape=jax.ShapeDtypeStruct(x2.shape, F32),
        compiler_params=pltpu.CompilerParams(dimension_semantics=("parallel",),
                                             vmem_limit_bytes=VMEM_LIMIT),
        name="outproj",
    )(x2, yc, ys, yr, wc, ws, wr, g)


def _pad_rows(a, rows):
    return jnp.concatenate([a, jnp.zeros((rows - a.shape[0],) + a.shape[1:], a.dtype)], axis=0)


def kernel(x, pre_norm_g, post_norm_g, w_in, w_out, conv_dw, conv_dw_b, conv_ln_g, conv_ln_b, conv_pw, conv_pw_b, rwkv_mu, rwkv_w0, rwkv_w2, rwkv_a0, rwkv_a2, rwkv_kk_scale, rwkv_ka, rwkv_rk, rwkv_gn_g, rwkv_gn_b, rwkv_v0, rwkv_v1, rwkv_v2):
    bsz, tlen, dm = x.shape
    depth = w_in.shape[0]
    assert dm == D_MODEL and tlen % BLOCK_Q == 0 and tlen % CHUNK == 0
    n = bsz * tlen
    tm = 256 if n % 256 == 0 else BLOCK_Q
    tt = 256 if tlen % 256 == 0 else BLOCK_Q
    row = lambda a: a.reshape(1, -1)
    c_end = 3 * CONV_WIDTH
    s_end = c_end + 4 * SB_WIDTH
    r_end = s_end + 3 * RWKV_WIDTH
    x2 = x.reshape(n, dm)
    vfirst = None
    for l in range(depth):
        wi = w_in[l]
        wa_cols = jnp.concatenate([wi[:, r_end:r_end + 2 * LORA],
                                   jnp.zeros((dm, WA_WIDTH - 2 * LORA), wi.dtype)], axis=1)
        w_r = jnp.concatenate([wi[:, s_end:r_end], wa_cols, wi[:, r_end + 2 * LORA:]], axis=1).astype(BF16)
        pc, ps, pr = _inproj(x2, row(pre_norm_g[l]), wi[:, :c_end].astype(BF16),
                             wi[:, c_end:s_end].astype(BF16), w_r, tm)

        yc = _conv(pc, conv_dw[l], row(conv_dw_b[l]), row(conv_ln_g[l]), row(conv_ln_b[l]),
                   conv_pw[l].astype(BF16), row(conv_pw_b[l]), bsz, tlen, tt)
        ys = _sb(ps, bsz, tlen)

        mu = jnp.concatenate([rwkv_mu[l], jnp.zeros((WA_WIDTH - 2 * LORA,), F32)]).reshape(1, -1)
        w2 = _pad_rows(rwkv_w2[l], WA_WIDTH)
        a2 = _pad_rows(jnp.concatenate([jnp.zeros((LORA, RWKV_WIDTH), F32), rwkv_a2[l]], axis=0), WA_WIDTH)
        vres = None
        if l > 0:
            v1 = jnp.concatenate([rwkv_v1[l - 1], jnp.zeros((RWKV_WIDTH, LANES - LORA), F32)], axis=1)
            vres = (row(rwkv_v0[l - 1]), v1, _pad_rows(rwkv_v2[l - 1], LANES))
        res = _rwkv(pr.reshape(bsz, tlen, R_IN), vfirst, mu, row(rwkv_w0[l]), w2, row(rwkv_a0[l]), a2,
                    row(rwkv_kk_scale[l]), row(rwkv_ka[l]), row(rwkv_rk[l]), row(rwkv_gn_g[l]),
                    row(rwkv_gn_b[l]), vres, bsz, tlen)
        if l == 0:
            yr, vfirst = res
        else:
            yr = res
        yr = yr.reshape(n, RWKV_WIDTH)

        wo = w_out[l].astype(BF16)
        x2 = _outproj(x2, yc, ys, yr, wo[:CONV_WIDTH], wo[CONV_WIDTH:CONV_WIDTH + SB_WIDTH],
                      wo[CONV_WIDTH + SB_WIDTH:], row(post_norm_g[l]), tm)
    return x2.reshape(bsz, tlen, dm)
```

```python
import functools

import jax
import jax.numpy as jnp
from jax import lax
from jax.experimental import pallas as pl
from jax.experimental.pallas import tpu as pltpu

D_MODEL = 1024
HEAD_DIM = 64
CONV_WIDTH = 256
SB_WIDTH = 384
RWKV_WIDTH = 384
CONV_TAPS = 31
LORA = 32
BLOCK_Q = 128
RMS_EPS = 1e-6
LN_EPS = 1e-5
GN_EPS = 64e-5

LANES = 128
SUBLANES = 8
PAIRS = RWKV_WIDTH // LANES
CHUNK = 64
RWKV_SEQS = 2
WA_WIDTH = LANES
R_IN = 3 * RWKV_WIDTH + WA_WIDTH + RWKV_WIDTH
SHIFT_COLS = 3 * RWKV_WIDTH + WA_WIDTH
VMEM_LIMIT = 56 * 1024 * 1024

F32 = jnp.float32
BF16 = jnp.bfloat16


def _dot1(a, b):
    return jnp.dot(a.astype(BF16), b.astype(BF16), preferred_element_type=F32)


def _split(a):
    hi = a.astype(BF16)
    lo = (a - hi.astype(F32)).astype(BF16)
    return hi, lo


def _hilo(w):
    hi, lo = _split(w)
    return jnp.stack([hi, lo])


def _dot3(a, w_ref):
    ah, al = _split(a)
    d = functools.partial(jnp.dot, preferred_element_type=F32)
    return d(ah, w_ref[0]) + (d(ah, w_ref[1]) + d(al, w_ref[0]))


def _dot2_exact_rhs(a, b_bf16):
    ah, al = _split(a)
    d = functools.partial(jnp.dot, preferred_element_type=F32)
    return d(ah, b_bf16) + d(al, b_bf16)


def _dot_nt1(a, b):
    return lax.dot_general(a.astype(BF16), b.astype(BF16), (((1,), (1,)), ((), ())),
                           preferred_element_type=F32)


def _sigmoid(x):
    return 1.0 / (1.0 + jnp.exp(-x))


def _silu(x):
    return x * _sigmoid(x)


def _softplus(x):
    return jnp.maximum(x, 0.0) + jnp.log(1.0 + jnp.exp(-jnp.abs(x)))


def _inproj_kernel(x_ref, g_ref, wc_ref, ws_ref, wr_ref, oc_ref, os_ref, or_ref):
    x = x_ref[...]
    ms = jnp.mean(x * x, axis=-1, keepdims=True)
    h = (x * lax.rsqrt(ms + RMS_EPS) * g_ref[...]).astype(BF16)
    for w_ref, o_ref in ((wc_ref, oc_ref), (ws_ref, os_ref), (wr_ref, or_ref)):
        n = o_ref.shape[1]
        for c0 in range(0, n, 256):
            c1 = min(c0 + 256, n)
            o_ref[:, c0:c1] = jnp.dot(h, w_ref[:, c0:c1], preferred_element_type=F32)


def _inproj(x2, g, wc, ws, wr, tm):
    n = x2.shape[0]
    full = lambda a: pl.BlockSpec(a.shape, lambda i: (0, 0))
    return pl.pallas_call(
        _inproj_kernel,
        grid=(n // tm,),
        in_specs=[pl.BlockSpec((tm, D_MODEL), lambda i: (i, 0)), full(g), full(wc), full(ws), full(wr)],
        out_specs=[pl.BlockSpec((tm, wc.shape[1]), lambda i: (i, 0)),
                   pl.BlockSpec((tm, ws.shape[1]), lambda i: (i, 0)),
                   pl.BlockSpec((tm, wr.shape[1]), lambda i: (i, 0))],
        out_shape=[jax.ShapeDtypeStruct((n, wc.shape[1]), F32),
                   jax.ShapeDtypeStruct((n, ws.shape[1]), F32),
                   jax.ShapeDtypeStruct((n, wr.shape[1]), F32)],
        compiler_params=pltpu.CompilerParams(dimension_semantics=("parallel",),
                                             vmem_limit_bytes=VMEM_LIMIT),
        name="inproj",
    )(x2, g, wc, ws, wr)


CONV_HALO = 32
CONV_SUB = 64


def _conv_kernel(pc_ref, dw_ref, dwb_ref, lng_ref, lnb_ref, pw_ref, pwb_ref, o_ref, u_ref):
    tt = o_ref.shape[0]

    @pl.when(pl.program_id(1) == 0)
    def _():
        u_ref[0:CONV_HALO, :] = jnp.zeros((CONV_HALO, CONV_WIDTH), F32)

    u_ref[CONV_HALO:CONV_HALO + tt, :] = pc_ref[:, 0:CONV_WIDTH] * _sigmoid(pc_ref[:, CONV_WIDTH:2 * CONV_WIDTH])
    for r0 in range(0, tt, CONV_SUB):
        acc = jnp.broadcast_to(dwb_ref[...], (CONV_SUB, CONV_WIDTH))
        for b in range(SUBLANES):
            part = None
            for a in range((CONV_TAPS - 1 - b) // SUBLANES + 1):
                tap = CONV_TAPS - 1 - (SUBLANES * a + b)
                s = CONV_HALO + r0 - SUBLANES * (a + 1)
                term = dw_ref[tap:tap + 1, :] * u_ref[s:s + CONV_SUB + SUBLANES, :]
                part = term if part is None else part + term
            acc = acc + part[SUBLANES - b:SUBLANES - b + CONV_SUB, :]
        mean = jnp.mean(acc, axis=-1, keepdims=True)
        d = acc - mean
        var = jnp.mean(d * d, axis=-1, keepdims=True)
        y = _silu(d * lax.rsqrt(var + LN_EPS) * lng_ref[...] + lnb_ref[...])
        y = _dot1(y, pw_ref[...]) + pwb_ref[...]
        o_ref[r0:r0 + CONV_SUB, :] = y * _silu(pc_ref[r0:r0 + CONV_SUB, 2 * CONV_WIDTH:3 * CONV_WIDTH])
    u_ref[0:CONV_HALO, :] = u_ref[tt:tt + CONV_HALO, :]


def _conv(pc, dw, dwb, lng, lnb, pw, pwb, bsz, tlen, tt):
    nt = tlen // tt
    full = lambda a: pl.BlockSpec(a.shape, lambda b, t: (0, 0))
    return pl.pallas_call(
        _conv_kernel,
        grid=(bsz, nt),
        in_specs=[pl.BlockSpec((tt, 3 * CONV_WIDTH), lambda b, t: (b * nt + t, 0)),
                  full(dw), full(dwb), full(lng), full(lnb), full(pw), full(pwb)],
        out_specs=pl.BlockSpec((tt, CONV_WIDTH), lambda b, t: (b * nt + t, 0)),
        out_shape=jax.ShapeDtypeStruct((bsz * tlen, CONV_WIDTH), F32),
        scratch_shapes=[pltpu.VMEM((CONV_HALO + tt, CONV_WIDTH), F32)],
        compiler_params=pltpu.CompilerParams(dimension_semantics=("parallel", "arbitrary"),
                                             vmem_limit_bytes=VMEM_LIMIT),
        name="conv",
    )(pc, dw, dwb, lng, lnb, pw, pwb)


SB_DEAD_LOG = -150.0
SB_EAGER = 2


def _sb_kernel(q_ref, k_ref, v_ref, g_ref, o_ref, carry_ref, acc_ref):
    i = pl.program_id(1)
    bq = BLOCK_Q
    lane = lax.broadcasted_iota(jnp.int32, (bq, LANES), 1)
    head0 = lane < HEAD_DIM
    row = lax.broadcasted_iota(jnp.int32, (LANES, 2 * LANES), 0)
    col = lax.broadcasted_iota(jnp.int32, (LANES, 2 * LANES), 1)
    cs = jnp.where((col >= LANES) | (row > col), 1.0, 0.0).astype(BF16)
    pairs = range(PAIRS)
    sls = [slice(p * LANES, (p + 1) * LANES) for p in pairs]

    def stacked_q(p):
        q = q_ref[:, sls[p]] * (HEAD_DIM ** -0.5)
        return jnp.concatenate([jnp.where(head0, q, 0.0), jnp.where(head0, 0.0, q)], axis=0).astype(BF16)

    def process(js, first):
        starts = [pl.multiple_of(j * bq, bq) for j in js]
        items = [(n, p) for n in range(len(js)) for p in pairs]
        qs = [stacked_q(p) for p in pairs]
        zs = {(n, p): _mm_nt(qs[p], k_ref[pl.ds(starts[n], bq), sls[p]].astype(BF16)) for n, p in items}
        if first:
            qi = lax.broadcasted_iota(jnp.int32, (2 * bq, bq), 0) % bq
            ki = lax.broadcasted_iota(jnp.int32, (2 * bq, bq), 1)
            causal = ki < qi
        log_betas, sums = {}, {}
        for n, p in items:
            z = zs[n, p]
            t = jnp.log(1.0 + jnp.exp(-jnp.abs(z)))
            log_keep = -jnp.maximum(z, 0.0) - t
            log_betas[n, p] = jnp.minimum(z, 0.0) - t
            if first and n == 0:
                log_keep = jnp.where(causal, log_keep, 0.0)
            sums[n, p] = _dot2_exact_rhs(log_keep, cs)
        live = None
        pvs = {}
        for p in pairs:
            carry = None if first else carry_ref[p]
            for n in range(len(js)):
                rest = sums[n, p][:, :LANES]
                if carry is not None:
                    rest = rest + carry
                att = jnp.exp(log_betas[n, p] + rest)
                if first and n == 0:
                    att = jnp.where(causal, att, 0.0)
                row_sum = sums[n, p][:, LANES:]
                carry = row_sum if carry is None else carry + row_sum
                pvs[n, p] = _dot1(att, v_ref[pl.ds(starts[n], bq), sls[p]])
            carry_ref[p] = carry
            m = jnp.max(carry)
            live = m if live is None else jnp.maximum(live, m)
        for p in pairs:
            tot = pvs[0, p]
            for n in range(1, len(js)):
                tot = tot + pvs[n, p]
            if first:
                acc_ref[p] = tot
            else:
                acc_ref[p] += tot
        return live

    def eager(_):
        return jnp.int32(SB_EAGER), process([i - n for n in range(SB_EAGER + 1)], True)

    def diagonal_only(_):
        return jnp.int32(0), process([i], True)

    done, live = lax.cond(i >= SB_EAGER, eager, diagonal_only, None)

    def cond(c):
        return (c[0] < i) & (c[1] > SB_DEAD_LOG)

    def body(c):
        return c[0] + 1, process([i - 1 - c[0]], False)

    lax.while_loop(cond, body, (done, live))
    for p in range(PAIRS):
        sl = slice(p * LANES, (p + 1) * LANES)
        o = jnp.where(head0, acc_ref[p, :bq], acc_ref[p, bq:])
        o_ref[:, sl] = o * _silu(g_ref[:, sl])


def _sb(ps, bsz, tlen):
    nq = tlen // BLOCK_Q
    return pl.pallas_call(
        _sb_kernel,
        grid=(bsz, nq),
        in_specs=[pl.BlockSpec((BLOCK_Q, SB_WIDTH), lambda b, i: (b * nq + i, 0)),
                  pl.BlockSpec((tlen, SB_WIDTH), lambda b, i: (b, 1)),
                  pl.BlockSpec((tlen, SB_WIDTH), lambda b, i: (b, 2)),
                  pl.BlockSpec((BLOCK_Q, SB_WIDTH), lambda b, i: (b * nq + i, 3))],
        out_specs=pl.BlockSpec((BLOCK_Q, SB_WIDTH), lambda b, i: (b * nq + i, 0)),
        out_shape=jax.ShapeDtypeStruct((bsz * tlen, SB_WIDTH), F32),
        scratch_shapes=[pltpu.VMEM((PAIRS, 2 * BLOCK_Q, LANES), F32),
                        pltpu.VMEM((PAIRS, 2 * BLOCK_Q, LANES), F32)],
        compiler_params=pltpu.CompilerParams(dimension_semantics=("parallel", "arbitrary"),
                                             vmem_limit_bytes=VMEM_LIMIT),
        name="stickbreak",
    )(ps, ps, ps, ps)


def _stack_heads(a, head0):
    return jnp.concatenate([jnp.where(head0, a, 0.0), jnp.where(head0, 0.0, a)], axis=0)


def _rwkv_kernel(*refs, has_vres):
    if has_vres:
        (pr_ref, vf_ref, mu_ref, w0_ref, w2_ref, a0_ref, a2_ref, kks_ref, ka_ref, rk_ref, gng_ref, gnb_ref,
         v0_ref, v1_ref, v2_ref, y_ref, xs_ref, st_ref) = refs
    else:
        (pr_ref, mu_ref, w0_ref, w2_ref, a0_ref, a2_ref, kks_ref, ka_ref, rk_ref, gng_ref, gnb_ref,
         y_ref, vf_out_ref, xs_ref, st_ref) = refs
    c = CHUNK
    w = RWKV_WIDTH
    nb = pr_ref.shape[0]

    @pl.when(pl.program_id(1) == 0)
    def _():
        xs_ref[:, 0:8, :] = jnp.zeros((nb, 8, SHIFT_COLS), F32)
        st_ref[...] = jnp.zeros(st_ref.shape, F32)

    lane = lax.broadcasted_iota(jnp.int32, (c, LANES), 1)
    head0 = lane < HEAD_DIM
    ri = lax.broadcasted_iota(jnp.int32, (LANES, LANES), 0)
    ci = lax.broadcasted_iota(jnp.int32, (LANES, LANES), 1)
    head_ones = jnp.where((ri < HEAD_DIM) == (ci < HEAD_DIM), 1.0, 0.0).astype(BF16)
    lower = ri > ci
    lower_eq = ri >= ci
    eye = jnp.where(ri == ci, 1.0, 0.0)
    rc = lax.broadcasted_iota(jnp.int32, (c, c), 0)
    cc = lax.broadcasted_iota(jnp.int32, (c, c), 1)
    cum = jnp.where(rc >= cc, 1.0, 0.0).astype(BF16)

    seqs = []
    for b in range(nb):
        x = pr_ref[b, :, 0:SHIFT_COLS]
        xs_ref[b, 8:8 + c, :] = x
        prev = xs_ref[b, 7:7 + c, :]
        xs_ref[b, 7:8, :] = x[c - 1:c, :]
        f = x + (prev - x) * mu_ref[...]
        r = f[:, 0:w]
        k = f[:, w:2 * w]
        v = f[:, 2 * w:3 * w]
        wa = f[:, 3 * w:3 * w + WA_WIDTH]
        w_pre = w0_ref[...] + _dot3(jnp.tanh(wa), w2_ref)
        logw = -jnp.exp(-_softplus(-w_pre) - 0.5)
        a_lr = _sigmoid(a0_ref[...] + _dot3(wa, a2_ref))
        if has_vres:
            gate = _sigmoid(v0_ref[...] + _dot3(_dot3(v, v1_ref), v2_ref))
            v = v + (vf_ref[b] - v) * gate
        else:
            vf_out_ref[b] = v
        lw_hi, lw_lo = _split(logw)
        logp = _mm(cum, lw_hi) + _mm(cum, lw_lo)
        seqs.append(dict(r=r, v=v, a_lr=a_lr, p=jnp.exp(logp), p_prev=jnp.exp(logp - logw),
                         inv_p=jnp.exp(-logp), kk=k * kks_ref[...],
                         k2=k * (1.0 + (a_lr - 1.0) * ka_ref[...])))

    chains = []
    for b, s in enumerate(seqs):
        for pr in range(PAIRS):
            sl = slice(pr * LANES, (pr + 1) * LANES)
            kk_p = s["kk"][:, sl]
            n2 = _dot2_exact_rhs(kk_p * kk_p, head_ones)
            kk_p = kk_p / jnp.maximum(jnp.sqrt(n2), 1e-12)
            a_t = _stack_heads(-kk_p * s["p_prev"][:, sl], head0).astype(BF16)
            b_f = _stack_heads(kk_p * s["a_lr"][:, sl] * s["inv_p"][:, sl], head0)
            k_f = _stack_heads(s["k2"][:, sl] * s["inv_p"][:, sl], head0)
            r_t = _stack_heads(s["r"][:, sl] * s["p"][:, sl], head0).astype(BF16)
            pe = s["p"][c - 1:c, sl]
            ch = dict(b=b, sl=sl, idx=b * PAIRS + pr, a_t=a_t, r_t=r_t, pe=pe,
                      v_s=_stack_heads(s["v"][:, sl], head0).astype(BF16),
                      be_ke=jnp.concatenate([(b_f * pe).T, (k_f * pe).T], axis=1).astype(BF16),
                      rkk=s["r"][:, sl] * s["k2"][:, sl] * rk_ref[:, sl], v=s["v"][:, sl])
            ch["gram"] = _mm_nt(jnp.concatenate([a_t, r_t], axis=0),
                                jnp.concatenate([b_f, k_f], axis=0).astype(BF16))
            chains.append(ch)

    for ch in chains:
        gram = ch.pop("gram")
        ab = jnp.where(lower, gram[:LANES, :LANES], 0.0)
        ak = jnp.where(lower, gram[:LANES, LANES:], 0.0).astype(BF16)
        ch["rb_rk"] = jnp.concatenate([jnp.where(lower_eq, gram[LANES:, :LANES], 0.0),
                                       jnp.where(lower_eq, gram[LANES:, LANES:], 0.0)], axis=1).astype(BF16)
        ch["akv"] = _mm(ak, ch["v_s"])
        g = st_ref[ch["idx"]]
        ch["g"] = g
        ch["ar_g"] = _mm(jnp.concatenate([ch["a_t"], ch["r_t"]], axis=0), g.astype(BF16))
        ch["s"] = eye + ab
        ch["pw"] = ab.astype(BF16)
    for ch in chains:
        ch["pw"] = _mm(ch["pw"], ch["pw"]).astype(BF16)
    for _ in range(4):
        for ch in chains:
            res = _mm(ch["pw"], jnp.concatenate([ch["s"].astype(BF16), ch["pw"]], axis=1))
            ch["s"] = ch["s"] + res[:, :LANES]
            ch["pw"] = res[:, LANES:].astype(BF16)
    for ch in chains:
        ch["s"] = (ch["s"] + _mm(ch["pw"], ch["s"].astype(BF16))).astype(BF16)
    for ch in chains:
        x = ch["ar_g"][:LANES] + ch["akv"]
        ch["u"] = _mm(ch["s"], x.astype(BF16)).astype(BF16)
    for ch in chains:
        lhs = jnp.concatenate([ch["be_ke"], ch["rb_rk"]], axis=0)
        ch["fin"] = _mm(lhs, jnp.concatenate([ch["u"], ch["v_s"]], axis=0))
    for ch in chains:
        pcol = jnp.broadcast_to(ch["pe"], (LANES, LANES)).T
        st_ref[ch["idx"]] = pcol * ch["g"] + ch["fin"][:LANES]
        y_s = ch["ar_g"][LANES:] + ch["fin"][LANES:]
        ch["y"] = y_s[:c] + y_s[c:]
        ch["mean"] = _dot2_exact_rhs(ch["y"], head_ones) * (1.0 / HEAD_DIM)
    for ch in chains:
        d = ch["y"] - ch["mean"]
        ch["d"] = d
        ch["var"] = _dot2_exact_rhs(d * d, head_ones) * (1.0 / HEAD_DIM)
        ch["bonus"] = _dot2_exact_rhs(ch["rkk"], head_ones)
    for ch in chains:
        b, sl = ch["b"], ch["sl"]
        y = ch["d"] * lax.rsqrt(ch["var"] + GN_EPS) * gng_ref[:, sl] + gnb_ref[:, sl]
        y = y + ch["bonus"] * ch["v"]
        y_ref[b, :, sl] = y * _silu(pr_ref[b, :, SHIFT_COLS + sl.start:SHIFT_COLS + sl.stop])


def _mm(a, b):
    return jnp.dot(a, b, preferred_element_type=F32)


def _mm_nt(a, b):
    return lax.dot_general(a, b, (((1,), (1,)), ((), ())), preferred_element_type=F32)


def _rwkv(pr, vfirst, mu, w0, w2, a0, a2, kks, ka, rk, gng, gnb, vres, bsz, tlen):
    nc = tlen // CHUNK
    nb = RWKV_SEQS if bsz % RWKV_SEQS == 0 else 1
    has_vres = vres is not None
    full = lambda a: pl.BlockSpec(a.shape, lambda b, t: (0,) * a.ndim)
    row_spec = lambda width: pl.BlockSpec((nb, CHUNK, width), lambda b, t: (b, t, 0))
    params = [mu, w0, w2, a0, a2, kks, ka, rk, gng, gnb]
    in_specs = [row_spec(R_IN)]
    args = [pr]
    if has_vres:
        in_specs.append(row_spec(RWKV_WIDTH))
        args.append(vfirst)
        params = params + list(vres)
    in_specs += [full(a) for a in params]
    args += params
    y_shape = jax.ShapeDtypeStruct((bsz, tlen, RWKV_WIDTH), F32)
    if has_vres:
        out_specs, out_shape = row_spec(RWKV_WIDTH), y_shape
    else:
        out_specs, out_shape = [row_spec(RWKV_WIDTH), row_spec(RWKV_WIDTH)], [y_shape, y_shape]
    return pl.pallas_call(
        functools.partial(_rwkv_kernel, has_vres=has_vres),
        grid=(bsz // nb, nc),
        in_specs=in_specs,
        out_specs=out_specs,
        out_shape=out_shape,
        scratch_shapes=[pltpu.VMEM((nb, 8 + CHUNK, SHIFT_COLS), F32),
                        pltpu.VMEM((nb * PAIRS, LANES, LANES), F32)],
        compiler_params=pltpu.CompilerParams(dimension_semantics=("parallel", "arbitrary"),
                                             vmem_limit_bytes=VMEM_LIMIT),
        name="rwkv7_vres" if has_vres else "rwkv7",
    )(*args)


def _outproj_kernel(x_ref, yc_ref, ys_ref, yr_ref, wc_ref, ws_ref, wr_ref, g_ref, o_ref):
    m = _dot1(yc_ref[...], wc_ref[...]) + _dot1(ys_ref[...], ws_ref[...]) + _dot1(yr_ref[...], wr_ref[...])
    ms = jnp.mean(m * m, axis=-1, keepdims=True)
    o_ref[...] = x_ref[...] + m * lax.rsqrt(ms + RMS_EPS) * g_ref[...]


def _outproj(x2, yc, ys, yr, wc, ws, wr, g, tm):
    n = x2.shape[0]
    full = lambda a: pl.BlockSpec(a.shape, lambda i: (0, 0))
    rows = lambda a: pl.BlockSpec((tm, a.shape[1]), lambda i: (i, 0))
    return pl.pallas_call(
        _outproj_kernel,
        grid=(n // tm,),
        in_specs=[rows(x2), rows(yc), rows(ys), rows(yr), full(wc), full(ws), full(wr), full(g)],
        out_specs=rows(x2),
        out_shape=jax.ShapeDtypeStruct(x2.shape, F32),
        compiler_params=pltpu.CompilerParams(dimension_semantics=("parallel",),
                                             vmem_limit_bytes=VMEM_LIMIT),
        name="outproj",
    )(x2, yc, ys, yr, wc, ws, wr, g)


def _pad_rows(a, rows):
    return jnp.concatenate([a, jnp.zeros((rows - a.shape[0],) + a.shape[1:], a.dtype)], axis=0)


def kernel(x, pre_norm_g, post_norm_g, w_in, w_out, conv_dw, conv_dw_b, conv_ln_g, conv_ln_b, conv_pw, conv_pw_b, rwkv_mu, rwkv_w0, rwkv_w2, rwkv_a0, rwkv_a2, rwkv_kk_scale, rwkv_ka, rwkv_rk, rwkv_gn_g, rwkv_gn_b, rwkv_v0, rwkv_v1, rwkv_v2):
    bsz, tlen, dm = x.shape
    depth = w_in.shape[0]
    assert dm == D_MODEL and tlen % BLOCK_Q == 0 and tlen % CHUNK == 0
    n = bsz * tlen
    tm = 256 if n % 256 == 0 else BLOCK_Q
    tt = 256 if tlen % 256 == 0 else BLOCK_Q
    row = lambda a: a.reshape(1, -1)
    c_end = 3 * CONV_WIDTH
    s_end = c_end + 4 * SB_WIDTH
    r_end = s_end + 3 * RWKV_WIDTH
    x2 = x.reshape(n, dm)
    vfirst = None
    for l in range(depth):
        wi = w_in[l]
        wa_cols = jnp.concatenate([wi[:, r_end:r_end + 2 * LORA],
                                   jnp.zeros((dm, WA_WIDTH - 2 * LORA), wi.dtype)], axis=1)
        w_r = jnp.concatenate([wi[:, s_end:r_end], wa_cols, wi[:, r_end + 2 * LORA:]], axis=1).astype(BF16)
        pc, ps, pr = _inproj(x2, row(pre_norm_g[l]), wi[:, :c_end].astype(BF16),
                             wi[:, c_end:s_end].astype(BF16), w_r, tm)

        yc = _conv(pc, conv_dw[l], row(conv_dw_b[l]), row(conv_ln_g[l]), row(conv_ln_b[l]),
                   conv_pw[l].astype(BF16), row(conv_pw_b[l]), bsz, tlen, tt)
        ys = _sb(ps, bsz, tlen)

        mu = jnp.concatenate([rwkv_mu[l], jnp.zeros((WA_WIDTH - 2 * LORA,), F32)]).reshape(1, -1)
        w2 = _hilo(_pad_rows(rwkv_w2[l], WA_WIDTH))
        a2 = _hilo(_pad_rows(jnp.concatenate([jnp.zeros((LORA, RWKV_WIDTH), F32), rwkv_a2[l]], axis=0),
                             WA_WIDTH))
        vres = None
        if l > 0:
            v1 = jnp.concatenate([rwkv_v1[l - 1], jnp.zeros((RWKV_WIDTH, LANES - LORA), F32)], axis=1)
            vres = (row(rwkv_v0[l - 1]), _hilo(v1), _hilo(_pad_rows(rwkv_v2[l - 1], LANES)))
        res = _rwkv(pr.reshape(bsz, tlen, R_IN), vfirst, mu, row(rwkv_w0[l]), w2, row(rwkv_a0[l]), a2,
                    row(rwkv_kk_scale[l]), row(rwkv_ka[l]), row(rwkv_rk[l]), row(rwkv_gn_g[l]),
                    row(rwkv_gn_b[l]), vres, bsz, tlen)
        if l == 0:
            yr, vfirst = res
        else:
            yr = res
        yr = yr.reshape(n, RWKV_WIDTH)

        wo = w_out[l].astype(BF16)
        x2 = _outproj(x2, yc, ys, yr, wo[:CONV_WIDTH], wo[CONV_WIDTH:CONV_WIDTH + SB_WIDTH],
                      wo[CONV_WIDTH + SB_WIDTH:], row(post_norm_g[l]), tm)
    return x2.reshape(bsz, tlen, dm)
```

```python
import functools

import jax
import jax.numpy as jnp
from jax import lax
from jax.experimental import pallas as pl
from jax.experimental.pallas import tpu as pltpu

D_MODEL = 1024
HEAD_DIM = 64
CONV_WIDTH = 256
SB_WIDTH = 384
RWKV_WIDTH = 384
CONV_TAPS = 31
LORA = 32
BLOCK_Q = 128
RMS_EPS = 1e-6
LN_EPS = 1e-5
GN_EPS = 64e-5

LANES = 128
SUBLANES = 8
PAIRS = RWKV_WIDTH // LANES
CHUNK = 64
assert CHUNK == HEAD_DIM
RWKV_SEQS = 8
WA_WIDTH = LANES
R_IN = 3 * RWKV_WIDTH + WA_WIDTH + RWKV_WIDTH
SHIFT_COLS = 3 * RWKV_WIDTH + WA_WIDTH
VMEM_LIMIT = 56 * 1024 * 1024

F32 = jnp.float32
BF16 = jnp.bfloat16


def _dot1(a, b):
    return jnp.dot(a.astype(BF16), b.astype(BF16), preferred_element_type=F32)


def _split(a):
    hi = a.astype(BF16)
    lo = (a - hi.astype(F32)).astype(BF16)
    return hi, lo


def _dot_nt1(a, b):
    return lax.dot_general(a.astype(BF16), b.astype(BF16), (((1,), (1,)), ((), ())),
                           preferred_element_type=F32)


def _sigmoid(x):
    return 1.0 / (1.0 + jnp.exp(-x))


def _silu(x):
    return x * _sigmoid(x)


def _softplus(x):
    return jnp.maximum(x, 0.0) + jnp.log(1.0 + jnp.exp(-jnp.abs(x)))


def _inproj_kernel(x_ref, g_ref, wc_ref, ws_ref, wr_ref, oc_ref, os_ref, or_ref):
    x = x_ref[...]
    ms = jnp.mean(x * x, axis=-1, keepdims=True)
    h = (x * lax.rsqrt(ms + RMS_EPS) * g_ref[...]).astype(BF16)
    for w_ref, o_ref in ((wc_ref, oc_ref), (ws_ref, os_ref), (wr_ref, or_ref)):
        n = o_ref.shape[1]
        for c0 in range(0, n, 256):
            c1 = min(c0 + 256, n)
            o_ref[:, c0:c1] = jnp.dot(h, w_ref[:, c0:c1], preferred_element_type=F32)


def _inproj(x2, g, wc, ws, wr, tm):
    n = x2.shape[0]
    full = lambda a: pl.BlockSpec(a.shape, lambda i: (0, 0))
    return pl.pallas_call(
        _inproj_kernel,
        grid=(n // tm,),
        in_specs=[pl.BlockSpec((tm, D_MODEL), lambda i: (i, 0)), full(g), full(wc), full(ws), full(wr)],
        out_specs=[pl.BlockSpec((tm, wc.shape[1]), lambda i: (i, 0)),
                   pl.BlockSpec((tm, ws.shape[1]), lambda i: (i, 0)),
                   pl.BlockSpec((tm, wr.shape[1]), lambda i: (i, 0))],
        out_shape=[jax.ShapeDtypeStruct((n, wc.shape[1]), F32),
                   jax.ShapeDtypeStruct((n, ws.shape[1]), F32),
                   jax.ShapeDtypeStruct((n, wr.shape[1]), F32)],
        compiler_params=pltpu.CompilerParams(dimension_semantics=("parallel",),
                                             vmem_limit_bytes=VMEM_LIMIT),
        name="inproj",
    )(x2, g, wc, ws, wr)


CONV_HALO = 32
CONV_SUB = 64


def _conv_kernel(pc_ref, dw_ref, dwb_ref, lng_ref, lnb_ref, pw_ref, pwb_ref, o_ref, u_ref):
    tt = o_ref.shape[0]

    @pl.when(pl.program_id(1) == 0)
    def _():
        u_ref[0:CONV_HALO, :] = jnp.zeros((CONV_HALO, CONV_WIDTH), F32)

    u_ref[CONV_HALO:CONV_HALO + tt, :] = pc_ref[:, 0:CONV_WIDTH] * _sigmoid(pc_ref[:, CONV_WIDTH:2 * CONV_WIDTH])
    for r0 in range(0, tt, CONV_SUB):
        acc = jnp.broadcast_to(dwb_ref[...], (CONV_SUB, CONV_WIDTH))
        for b in range(SUBLANES):
            part = None
            for a in range((CONV_TAPS - 1 - b) // SUBLANES + 1):
                tap = CONV_TAPS - 1 - (SUBLANES * a + b)
                s = CONV_HALO + r0 - SUBLANES * (a + 1)
                term = dw_ref[tap:tap + 1, :] * u_ref[s:s + CONV_SUB + SUBLANES, :]
                part = term if part is None else part + term
            acc = acc + part[SUBLANES - b:SUBLANES - b + CONV_SUB, :]
        mean = jnp.mean(acc, axis=-1, keepdims=True)
        d = acc - mean
        var = jnp.mean(d * d, axis=-1, keepdims=True)
        y = _silu(d * lax.rsqrt(var + LN_EPS) * lng_ref[...] + lnb_ref[...])
        y = _dot1(y, pw_ref[...]) + pwb_ref[...]
        o_ref[r0:r0 + CONV_SUB, :] = y * _silu(pc_ref[r0:r0 + CONV_SUB, 2 * CONV_WIDTH:3 * CONV_WIDTH])
    u_ref[0:CONV_HALO, :] = u_ref[tt:tt + CONV_HALO, :]


def _conv(pc, dw, dwb, lng, lnb, pw, pwb, bsz, tlen, tt):
    nt = tlen // tt
    full = lambda a: pl.BlockSpec(a.shape, lambda b, t: (0, 0))
    return pl.pallas_call(
        _conv_kernel,
        grid=(bsz, nt),
        in_specs=[pl.BlockSpec((tt, 3 * CONV_WIDTH), lambda b, t: (b * nt + t, 0)),
                  full(dw), full(dwb), full(lng), full(lnb), full(pw), full(pwb)],
        out_specs=pl.BlockSpec((tt, CONV_WIDTH), lambda b, t: (b * nt + t, 0)),
        out_shape=jax.ShapeDtypeStruct((bsz * tlen, CONV_WIDTH), F32),
        scratch_shapes=[pltpu.VMEM((CONV_HALO + tt, CONV_WIDTH), F32)],
        compiler_params=pltpu.CompilerParams(dimension_semantics=("parallel", "arbitrary"),
                                             vmem_limit_bytes=VMEM_LIMIT),
        name="conv",
    )(pc, dw, dwb, lng, lnb, pw, pwb)


SB_DEAD_LOG = -150.0
SB_EAGER = 2


def _sb_kernel(q_ref, k_ref, v_ref, g_ref, o_ref, carry_ref, acc_ref):
    i = pl.program_id(1)
    bq = BLOCK_Q
    lane = lax.broadcasted_iota(jnp.int32, (bq, LANES), 1)
    head0 = lane < HEAD_DIM
    row = lax.broadcasted_iota(jnp.int32, (LANES, 2 * LANES), 0)
    col = lax.broadcasted_iota(jnp.int32, (LANES, 2 * LANES), 1)
    cs = jnp.where((col >= LANES) | (row > col), 1.0, 0.0).astype(BF16)
    pairs = range(PAIRS)
    sls = [slice(p * LANES, (p + 1) * LANES) for p in pairs]

    def stacked_q(p):
        q = q_ref[:, sls[p]] * (HEAD_DIM ** -0.5)
        return jnp.concatenate([jnp.where(head0, q, 0.0), jnp.where(head0, 0.0, q)], axis=0).astype(BF16)

    def process(js, first):
        starts = [pl.multiple_of(j * bq, bq) for j in js]
        items = [(n, p) for n in range(len(js)) for p in pairs]
        qs = [stacked_q(p) for p in pairs]
        zs = {(n, p): _mm_nt(qs[p], k_ref[pl.ds(starts[n], bq), sls[p]].astype(BF16)) for n, p in items}
        if first:
            qi = lax.broadcasted_iota(jnp.int32, (2 * bq, bq), 0) % bq
            ki = lax.broadcasted_iota(jnp.int32, (2 * bq, bq), 1)
            causal = ki < qi
        log_betas, sums = {}, {}
        for n, p in items:
            z = zs[n, p]
            t = jnp.log(1.0 + jnp.exp(-jnp.abs(z)))
            log_keep = -jnp.maximum(z, 0.0) - t
            log_betas[n, p] = jnp.minimum(z, 0.0) - t
            if first and n == 0:
                log_keep = jnp.where(causal, log_keep, 0.0)
            sums[n, p] = _dot1(log_keep, cs)
        live = None
        pvs = {}
        for p in pairs:
            carry = None if first else carry_ref[p]
            for n in range(len(js)):
                rest = sums[n, p][:, :LANES]
                if carry is not None:
                    rest = rest + carry
                att = jnp.exp(log_betas[n, p] + rest)
                if first and n == 0:
                    att = jnp.where(causal, att, 0.0)
                row_sum = sums[n, p][:, LANES:]
                carry = row_sum if carry is None else carry + row_sum
                pvs[n, p] = _dot1(att, v_ref[pl.ds(starts[n], bq), sls[p]])
            carry_ref[p] = carry
            m = jnp.max(carry)
            live = m if live is None else jnp.maximum(live, m)
        for p in pairs:
            tot = pvs[0, p]
            for n in range(1, len(js)):
                tot = tot + pvs[n, p]
            if first:
                acc_ref[p] = tot
            else:
                acc_ref[p] += tot
        return live

    def eager(_):
        return jnp.int32(SB_EAGER), process([i - n for n in range(SB_EAGER + 1)], True)

    def diagonal_only(_):
        return jnp.int32(0), process([i], True)

    done, live = lax.cond(i >= SB_EAGER, eager, diagonal_only, None)

    def cond(c):
        return (c[0] < i) & (c[1] > SB_DEAD_LOG)

    def body(c):
        return c[0] + 1, process([i - 1 - c[0]], False)

    lax.while_loop(cond, body, (done, live))
    for p in range(PAIRS):
        sl = slice(p * LANES, (p + 1) * LANES)
        o = jnp.where(head0, acc_ref[p, :bq], acc_ref[p, bq:])
        o_ref[:, sl] = o * _silu(g_ref[:, sl])


def _sb(ps, bsz, tlen):
    nq = tlen // BLOCK_Q
    return pl.pallas_call(
        _sb_kernel,
        grid=(bsz, nq),
        in_specs=[pl.BlockSpec((BLOCK_Q, SB_WIDTH), lambda b, i: (b * nq + i, 0)),
                  pl.BlockSpec((tlen, SB_WIDTH), lambda b, i: (b, 1)),
                  pl.BlockSpec((tlen, SB_WIDTH), lambda b, i: (b, 2)),
                  pl.BlockSpec((BLOCK_Q, SB_WIDTH), lambda b, i: (b * nq + i, 3))],
        out_specs=pl.BlockSpec((BLOCK_Q, SB_WIDTH), lambda b, i: (b * nq + i, 0)),
        out_shape=jax.ShapeDtypeStruct((bsz * tlen, SB_WIDTH), F32),
        scratch_shapes=[pltpu.VMEM((PAIRS, 2 * BLOCK_Q, LANES), F32),
                        pltpu.VMEM((PAIRS, 2 * BLOCK_Q, LANES), F32)],
        compiler_params=pltpu.CompilerParams(dimension_semantics=("parallel", "arbitrary"),
                                             vmem_limit_bytes=VMEM_LIMIT),
        name="stickbreak",
    )(ps, ps, ps, ps)


def _stack_heads(a, head0):
    return jnp.concatenate([jnp.where(head0, a, 0.0), jnp.where(head0, 0.0, a)], axis=0)


def _rwkv_kernel(*refs, has_vres):
    if has_vres:
        (pr_ref, vf_ref, mu_ref, w0_ref, w2_ref, a0_ref, a2_ref, kks_ref, ka_ref, rk_ref, gng_ref, gnb_ref,
         v0_ref, v1_ref, v2_ref, y_ref, xs_ref, st_ref) = refs
    else:
        (pr_ref, mu_ref, w0_ref, w2_ref, a0_ref, a2_ref, kks_ref, ka_ref, rk_ref, gng_ref, gnb_ref,
         y_ref, vf_out_ref, xs_ref, st_ref) = refs
    c = CHUNK
    w = RWKV_WIDTH
    nb = pr_ref.shape[0]

    @pl.when(pl.program_id(1) == 0)
    def _():
        xs_ref[:, 0:8, :] = jnp.zeros((nb, 8, SHIFT_COLS), F32)
        st_ref[...] = jnp.zeros(st_ref.shape, F32)

    lane = lax.broadcasted_iota(jnp.int32, (c, LANES), 1)
    head0 = lane < HEAD_DIM
    ri = lax.broadcasted_iota(jnp.int32, (LANES, LANES), 0)
    ci = lax.broadcasted_iota(jnp.int32, (LANES, LANES), 1)
    same_head = (ri < HEAD_DIM) == (ci < HEAD_DIM)
    head_ones = jnp.where(same_head, 1.0, 0.0).astype(BF16)
    ti = lax.broadcasted_iota(jnp.int32, (c, LANES), 0)
    ji = lane % HEAD_DIM
    lower = ti > ji
    lower_eq = ti >= ji
    eye = jnp.where(ti == ji, 1.0, 0.0)
    rc = lax.broadcasted_iota(jnp.int32, (c, c), 0)
    cc = lax.broadcasted_iota(jnp.int32, (c, c), 1)
    cum = jnp.where(rc >= cc, 1.0, 0.0).astype(BF16)

    def sequence(b):
        x = pr_ref[b, :, 0:SHIFT_COLS]
        xs_ref[b, 8:8 + c, :] = x
        prev = xs_ref[b, 7:7 + c, :]
        xs_ref[b, 7:8, :] = x[c - 1:c, :]
        f = x + (prev - x) * mu_ref[...]
        r = f[:, 0:w]
        k = f[:, w:2 * w]
        v = f[:, 2 * w:3 * w]
        wa = f[:, 3 * w:3 * w + WA_WIDTH]
        w_pre = w0_ref[...] + _dot1(jnp.tanh(wa), w2_ref[...])
        a_pre = a0_ref[...] + _dot1(wa, a2_ref[...])
        if has_vres:
            v_mid = _dot1(v, v1_ref[...])
        yield
        logw = -jnp.exp(-_softplus(-w_pre) - 0.5)
        a_lr = _sigmoid(a_pre)
        if has_vres:
            v = v + (vf_ref[b] - v) * _sigmoid(v0_ref[...] + _dot1(v_mid, v2_ref[...]))
        else:
            vf_out_ref[b] = v
        lw_hi, lw_lo = _split(logw)
        logp = _mm(cum, lw_hi) + _mm(cum, lw_lo)
        yield
        p = jnp.exp(logp)
        p_prev = jnp.exp(logp - logw)
        inv_p = jnp.exp(-logp)
        kk = k * kks_ref[...]
        k2 = k * (1.0 + (a_lr - 1.0) * ka_ref[...])
        sls = [slice(pr * LANES, (pr + 1) * LANES) for pr in range(PAIRS)]
        n2s = [_mm((kk[:, sl] * kk[:, sl]).astype(BF16), head_ones) for sl in sls]
        yield
        stack = lambda a: _stack_heads(a, head0).astype(BF16)
        chains = []
        for pr, sl in enumerate(sls):
            kk_p = kk[:, sl] / jnp.maximum(jnp.sqrt(n2s[pr]), 1e-12)
            a_n = -kk_p * p_prev[:, sl]
            b_n = kk_p * a_lr[:, sl] * inv_p[:, sl]
            k_n = k2[:, sl] * inv_p[:, sl]
            r_n = r[:, sl] * p[:, sl]
            pe = p[c - 1:c, sl]
            ar = jnp.concatenate([a_n, r_n], axis=0).astype(BF16)
            ch = dict(sl=sl, idx=b * PAIRS + pr, ar=ar, pe=pe, v_st=stack(v[:, sl]),
                      be_ke=jnp.concatenate([(b_n * pe).T, (k_n * pe).T], axis=1).astype(BF16),
                      rkk=(r[:, sl] * k2[:, sl] * rk_ref[:, sl]).astype(BF16), v=v[:, sl])
            ch["gram"] = _mm_nt(ar, jnp.concatenate([stack(b_n), stack(k_n)], axis=0))
            chains.append(ch)
        yield
        for ch in chains:
            gram = ch.pop("gram")
            ab = jnp.where(lower, gram[:c, :LANES], 0.0)
            ak = jnp.where(lower, gram[:c, LANES:], 0.0).astype(BF16)
            ch["rb_rk"] = jnp.concatenate([jnp.where(lower_eq, gram[c:, :LANES], 0.0),
                                           jnp.where(lower_eq, gram[c:, LANES:], 0.0)], axis=1).astype(BF16)
            ch["akv"] = _mm(ak, ch["v_st"])
            g = st_ref[ch["idx"]]
            ch["g"] = g
            ch["ar_g"] = _mm(ch["ar"], g.astype(BF16))
            ch["s"] = eye + ab
            ch["pw"] = _mm(ab.astype(BF16), stack(ab))
        yield
        for _ in range(4):
            for ch in chains:
                res = _mm(ch["pw"].astype(BF16), jnp.concatenate([stack(ch["s"]), stack(ch["pw"])], axis=1))
                ch["s"] = ch["s"] + res[:, :LANES]
                ch["pw"] = res[:, LANES:]
            yield
        for ch in chains:
            ch["s"] = ch["s"] + _mm(ch["pw"].astype(BF16), stack(ch["s"]))
        yield
        for ch in chains:
            xa = ch["ar_g"][:c] + ch["akv"]
            ch["u"] = _mm(ch["s"].astype(BF16), stack(xa))
        yield
        for ch in chains:
            u_st = stack(ch["u"])
            ch["inc"] = _mm(ch["be_ke"], jnp.concatenate([ch["u"].astype(BF16), ch["v"].astype(BF16)], axis=0))
            ch["y_in"] = _mm(ch["rb_rk"], jnp.concatenate([u_st, ch["v_st"]], axis=0))
            ch["bonus"] = _mm(ch["rkk"], head_ones)
        yield
        for ch in chains:
            pcol = jnp.broadcast_to(ch["pe"], (LANES, LANES)).T
            st_ref[ch["idx"]] = pcol * ch["g"] + jnp.where(same_head, ch["inc"], 0.0)
            ch["y"] = ch["ar_g"][c:] + ch["y_in"]
            ch["mean"] = _dot1(ch["y"], head_ones) * (1.0 / HEAD_DIM)
        yield
        for ch in chains:
            d = ch["y"] - ch["mean"]
            ch["d"] = d
            ch["var"] = _dot1(d * d, head_ones) * (1.0 / HEAD_DIM)
        yield
        for ch in chains:
            sl = ch["sl"]
            y = ch["d"] * lax.rsqrt(ch["var"] + GN_EPS) * gng_ref[:, sl] + gnb_ref[:, sl]
            y = y + ch["bonus"] * ch["v"]
            y_ref[b, :, sl] = y * _silu(pr_ref[b, :, SHIFT_COLS + sl.start:SHIFT_COLS + sl.stop])

    programs = [sequence(b) for b in range(nb)]
    while programs:
        programs = [prog for prog in programs if next(prog, True) is None]


def _mm(a, b):
    return jnp.dot(a, b, preferred_element_type=F32)


def _mm_nt(a, b):
    return lax.dot_general(a, b, (((1,), (1,)), ((), ())), preferred_element_type=F32)


def _rwkv(pr, vfirst, mu, w0, w2, a0, a2, kks, ka, rk, gng, gnb, vres, bsz, tlen):
    nc = tlen // CHUNK
    nb = RWKV_SEQS if bsz % RWKV_SEQS == 0 else 1
    has_vres = vres is not None
    full = lambda a: pl.BlockSpec(a.shape, lambda b, t: (0,) * a.ndim)
    row_spec = lambda width: pl.BlockSpec((nb, CHUNK, width), lambda b, t: (b, t, 0))
    params = [mu, w0, w2, a0, a2, kks, ka, rk, gng, gnb]
    in_specs = [row_spec(R_IN)]
    args = [pr]
    if has_vres:
        in_specs.append(row_spec(RWKV_WIDTH))
        args.append(vfirst)
        params = params + list(vres)
    in_specs += [full(a) for a in params]
    args += params
    y_shape = jax.ShapeDtypeStruct((bsz, tlen, RWKV_WIDTH), F32)
    if has_vres:
        out_specs, out_shape = row_spec(RWKV_WIDTH), y_shape
    else:
        out_specs, out_shape = [row_spec(RWKV_WIDTH), row_spec(RWKV_WIDTH)], [y_shape, y_shape]
    return pl.pallas_call(
        functools.partial(_rwkv_kernel, has_vres=has_vres),
        grid=(bsz // nb, nc),
        in_specs=in_specs,
        out_specs=out_specs,
        out_shape=out_shape,
        scratch_shapes=[pltpu.VMEM((nb, 8 + CHUNK, SHIFT_COLS), F32),
                        pltpu.VMEM((nb * PAIRS, LANES, LANES), F32)],
        compiler_params=pltpu.CompilerParams(dimension_semantics=("parallel", "arbitrary"),
                                             vmem_limit_bytes=VMEM_LIMIT),
        name="rwkv7_vres" if has_vres else "rwkv7",
    )(*args)


def _outproj_kernel(x_ref, yc_ref, ys_ref, yr_ref, wc_ref, ws_ref, wr_ref, g_ref, o_ref):
    m = _dot1(yc_ref[...], wc_ref[...]) + _dot1(ys_ref[...], ws_ref[...]) + _dot1(yr_ref[...], wr_ref[...])
    ms = jnp.mean(m * m, axis=-1, keepdims=True)
    o_ref[...] = x_ref[...] + m * lax.rsqrt(ms + RMS_EPS) * g_ref[...]


def _outproj(x2, yc, ys, yr, wc, ws, wr, g, tm):
    n = x2.shape[0]
    full = lambda a: pl.BlockSpec(a.shape, lambda i: (0, 0))
    rows = lambda a: pl.BlockSpec((tm, a.shape[1]), lambda i: (i, 0))
    return pl.pallas_call(
        _outproj_kernel,
        grid=(n // tm,),
        in_specs=[rows(x2), rows(yc), rows(ys), rows(yr), full(wc), full(ws), full(wr), full(g)],
        out_specs=rows(x2),
        out_shape=jax.ShapeDtypeStruct(x2.shape, F32),
        compiler_params=pltpu.CompilerParams(dimension_semantics=("parallel",),
                                             vmem_limit_bytes=VMEM_LIMIT),
        name="outproj",
    )(x2, yc, ys, yr, wc, ws, wr, g)


def _pad_rows(a, rows):
    return jnp.concatenate([a, jnp.zeros((rows - a.shape[0],) + a.shape[1:], a.dtype)], axis=0)


def kernel(x, pre_norm_g, post_norm_g, w_in, w_out, conv_dw, conv_dw_b, conv_ln_g, conv_ln_b, conv_pw, conv_pw_b, rwkv_mu, rwkv_w0, rwkv_w2, rwkv_a0, rwkv_a2, rwkv_kk_scale, rwkv_ka, rwkv_rk, rwkv_gn_g, rwkv_gn_b, rwkv_v0, rwkv_v1, rwkv_v2):
    bsz, tlen, dm = x.shape
    depth = w_in.shape[0]
    assert dm == D_MODEL and tlen % BLOCK_Q == 0 and tlen % CHUNK == 0
    n = bsz * tlen
    tm = 256 if n % 256 == 0 else BLOCK_Q
    tt = 256 if tlen % 256 == 0 else BLOCK_Q
    row = lambda a: a.reshape(1, -1)
    c_end = 3 * CONV_WIDTH
    s_end = c_end + 4 * SB_WIDTH
    r_end = s_end + 3 * RWKV_WIDTH
    x2 = x.reshape(n, dm)
    vfirst = None
    for l in range(depth):
        wi = w_in[l]
        wa_cols = jnp.concatenate([wi[:, r_end:r_end + 2 * LORA],
                                   jnp.zeros((dm, WA_WIDTH - 2 * LORA), wi.dtype)], axis=1)
        w_r = jnp.concatenate([wi[:, s_end:r_end], wa_cols, wi[:, r_end + 2 * LORA:]], axis=1).astype(BF16)
        pc, ps, pr = _inproj(x2, row(pre_norm_g[l]), wi[:, :c_end].astype(BF16),
                             wi[:, c_end:s_end].astype(BF16), w_r, tm)

        yc = _conv(pc, conv_dw[l], row(conv_dw_b[l]), row(conv_ln_g[l]), row(conv_ln_b[l]),
                   conv_pw[l].astype(BF16), row(conv_pw_b[l]), bsz, tlen, tt)
        ys = _sb(ps, bsz, tlen)

        mu = jnp.concatenate([rwkv_mu[l], jnp.zeros((WA_WIDTH - 2 * LORA,), F32)]).reshape(1, -1)
        w2 = _pad_rows(rwkv_w2[l], WA_WIDTH).astype(BF16)
        a2 = _pad_rows(jnp.concatenate([jnp.zeros((LORA, RWKV_WIDTH), F32), rwkv_a2[l]], axis=0),
                       WA_WIDTH).astype(BF16)
        vres = None
        if l > 0:
            v1 = jnp.concatenate([rwkv_v1[l - 1], jnp.zeros((RWKV_WIDTH, LANES - LORA), F32)], axis=1)
            vres = (row(rwkv_v0[l - 1]), v1.astype(BF16), _pad_rows(rwkv_v2[l - 1], LANES).astype(BF16))
        res = _rwkv(pr.reshape(bsz, tlen, R_IN), vfirst, mu, row(rwkv_w0[l]), w2, row(rwkv_a0[l]), a2,
                    row(rwkv_kk_scale[l]), row(rwkv_ka[l]), row(rwkv_rk[l]), row(rwkv_gn_g[l]),
                    row(rwkv_gn_b[l]), vres, bsz, tlen)
        if l == 0:
            yr, vfirst = res
        else:
            yr = res
        yr = yr.reshape(n, RWKV_WIDTH)

        wo = w_out[l].astype(BF16)
        x2 = _outproj(x2, yc, ys, yr, wo[:CONV_WIDTH], wo[CONV_WIDTH:CONV_WIDTH + SB_WIDTH],
                      wo[CONV_WIDTH + SB_WIDTH:], row(post_norm_g[l]), tm)
    return x2.reshape(bsz, tlen, dm)
```

```python
import functools

import jax
import jax.numpy as jnp
from jax import lax
from jax.experimental import pallas as pl
from jax.experimental.pallas import tpu as pltpu

D_MODEL = 1024
HEAD_DIM = 64
CONV_WIDTH = 256
SB_WIDTH = 384
RWKV_WIDTH = 384
CONV_TAPS = 31
LORA = 32
BLOCK_Q = 128
RMS_EPS = 1e-6
LN_EPS = 1e-5
GN_EPS = 64e-5

LANES = 128
SUBLANES = 8
PAIRS = RWKV_WIDTH // LANES
CHUNK = 64
assert CHUNK == HEAD_DIM
RWKV_SEQS = 8
WA_WIDTH = LANES
R_IN = 3 * RWKV_WIDTH + WA_WIDTH + RWKV_WIDTH
SHIFT_COLS = 3 * RWKV_WIDTH + WA_WIDTH
VMEM_LIMIT = 56 * 1024 * 1024

F32 = jnp.float32
BF16 = jnp.bfloat16


def _dot1(a, b):
    return jnp.dot(a.astype(BF16), b.astype(BF16), preferred_element_type=F32)


def _split(a):
    hi = a.astype(BF16)
    lo = (a - hi.astype(F32)).astype(BF16)
    return hi, lo


def _dot_nt1(a, b):
    return lax.dot_general(a.astype(BF16), b.astype(BF16), (((1,), (1,)), ((), ())),
                           preferred_element_type=F32)


def _sigmoid(x):
    return 1.0 / (1.0 + jnp.exp(-x))


def _silu(x):
    return x * _sigmoid(x)


def _softplus(x):
    return jnp.maximum(x, 0.0) + jnp.log(1.0 + jnp.exp(-jnp.abs(x)))


CONV_HALO = 32
CONV_SUB = 64
MXU_COLS = 256


def _proj_slabs(h, w_ref, o_ref, cols=None):
    lo, hi = cols if cols is not None else (0, w_ref.shape[1])

    def slab(c0, c1):
        o_ref[:, c0 - lo:c1 - lo] = jnp.dot(h, w_ref[:, c0:c1], preferred_element_type=F32).astype(o_ref.dtype)

    return [functools.partial(slab, c0, min(c0 + MXU_COLS, hi)) for c0 in range(lo, hi, MXU_COLS)]


def _inproj_kernel(x_ref, g_ref, wc_ref, ws_ref, wr_ref, dw_ref, dwb_ref, lng_ref, lnb_ref, pw_ref, pwb_ref,
                   yc_ref, qkv_ref, gs_ref, pr_ref, pc_ref, u_ref, *, tiles_per_seq):
    tt = x_ref.shape[0]

    @pl.when(pl.program_id(0) % tiles_per_seq == 0)
    def _():
        u_ref[0:CONV_HALO, :] = jnp.zeros((CONV_HALO, CONV_WIDTH), F32)

    x = x_ref[...]
    ms = jnp.mean(x * x, axis=-1, keepdims=True)
    h = (x * lax.rsqrt(ms + RMS_EPS) * g_ref[...]).astype(BF16)
    for job in _proj_slabs(h, wc_ref, pc_ref):
        job()
    u_ref[CONV_HALO:CONV_HALO + tt, :] = pc_ref[:, 0:CONV_WIDTH] * _sigmoid(pc_ref[:, CONV_WIDTH:2 * CONV_WIDTH])
    slabs = (_proj_slabs(h, ws_ref, qkv_ref, (0, 3 * SB_WIDTH))
             + _proj_slabs(h, ws_ref, gs_ref, (3 * SB_WIDTH, 4 * SB_WIDTH)) + _proj_slabs(h, wr_ref, pr_ref))
    groups = tt // CONV_SUB
    for r0 in range(0, tt, CONV_SUB):
        gi = r0 // CONV_SUB
        for job in slabs[gi * len(slabs) // groups:(gi + 1) * len(slabs) // groups]:
            job()
        acc = jnp.broadcast_to(dwb_ref[...], (CONV_SUB, CONV_WIDTH))
        for b in range(SUBLANES):
            part = None
            for a in range((CONV_TAPS - 1 - b) // SUBLANES + 1):
                tap = CONV_TAPS - 1 - (SUBLANES * a + b)
                s = CONV_HALO + r0 - SUBLANES * (a + 1)
                term = dw_ref[tap:tap + 1, :] * u_ref[s:s + CONV_SUB + SUBLANES, :]
                part = term if part is None else part + term
            acc = acc + part[SUBLANES - b:SUBLANES - b + CONV_SUB, :]
        mean = jnp.mean(acc, axis=-1, keepdims=True)
        d = acc - mean
        var = jnp.mean(d * d, axis=-1, keepdims=True)
        y = _silu(d * lax.rsqrt(var + LN_EPS) * lng_ref[...] + lnb_ref[...])
        y = _dot1(y, pw_ref[...]) + pwb_ref[...]
        y = y * _silu(pc_ref[r0:r0 + CONV_SUB, 2 * CONV_WIDTH:3 * CONV_WIDTH])
        yc_ref[r0:r0 + CONV_SUB, :] = y.astype(yc_ref.dtype)
    u_ref[0:CONV_HALO, :] = u_ref[tt:tt + CONV_HALO, :]


def _inproj(x2, g, wc, ws, wr, conv_params, tm, tlen):
    n = x2.shape[0]
    full = lambda a: pl.BlockSpec(a.shape, lambda i: (0, 0))
    rows = lambda width: pl.BlockSpec((tm, width), lambda i: (i, 0))
    widths = (CONV_WIDTH, 3 * SB_WIDTH, SB_WIDTH, wr.shape[1])
    dtypes = (BF16, BF16, F32, F32)
    return pl.pallas_call(
        functools.partial(_inproj_kernel, tiles_per_seq=tlen // tm),
        grid=(n // tm,),
        in_specs=[rows(D_MODEL), full(g), full(wc), full(ws), full(wr)] + [full(a) for a in conv_params],
        out_specs=[rows(wd) for wd in widths],
        out_shape=[jax.ShapeDtypeStruct((n, wd), dt) for wd, dt in zip(widths, dtypes)],
        scratch_shapes=[pltpu.VMEM((tm, wc.shape[1]), F32),
                        pltpu.VMEM((CONV_HALO + tm, CONV_WIDTH), F32)],
        compiler_params=pltpu.CompilerParams(dimension_semantics=("arbitrary",),
                                             vmem_limit_bytes=VMEM_LIMIT),
        name="inproj_conv",
    )(x2, g, wc, ws, wr, *conv_params)


SB_DEAD_LOG = -150.0
SB_EAGER = 2


def _sb_kernel(q_ref, k_ref, v_ref, g_ref, o_ref, carry_ref, acc_ref):
    i = pl.program_id(1)
    bq = BLOCK_Q
    lane = lax.broadcasted_iota(jnp.int32, (bq, LANES), 1)
    head0 = lane < HEAD_DIM
    row = lax.broadcasted_iota(jnp.int32, (LANES, 2 * LANES), 0)
    col = lax.broadcasted_iota(jnp.int32, (LANES, 2 * LANES), 1)
    cs = jnp.where((col >= LANES) | (row > col), 1.0, 0.0).astype(BF16)
    pairs = range(PAIRS)
    sls = [slice(p * LANES, (p + 1) * LANES) for p in pairs]

    def stacked_q(p):
        q = q_ref[:, sls[p]].astype(F32) * (HEAD_DIM ** -0.5)
        return jnp.concatenate([jnp.where(head0, q, 0.0), jnp.where(head0, 0.0, q)], axis=0).astype(BF16)

    def process(js, first):
        starts = [pl.multiple_of(j * bq, bq) for j in js]
        items = [(n, p) for n in range(len(js)) for p in pairs]
        qs = [stacked_q(p) for p in pairs]
        zs = {(n, p): _mm_nt(qs[p], k_ref[pl.ds(starts[n], bq), sls[p]]) for n, p in items}
        if first:
            qi = lax.broadcasted_iota(jnp.int32, (2 * bq, bq), 0) % bq
            ki = lax.broadcasted_iota(jnp.int32, (2 * bq, bq), 1)
            causal = ki < qi
        log_betas, sums = {}, {}
        for n, p in items:
            z = zs[n, p]
            t = jnp.log(1.0 + jnp.exp(-jnp.abs(z)))
            log_keep = -jnp.maximum(z, 0.0) - t
            log_betas[n, p] = jnp.minimum(z, 0.0) - t
            if first and n == 0:
                log_keep = jnp.where(causal, log_keep, 0.0)
            sums[n, p] = _dot1(log_keep, cs)
        live = None
        pvs = {}
        for p in pairs:
            carry = None if first else carry_ref[p]
            for n in range(len(js)):
                rest = sums[n, p][:, :LANES]
                if carry is not None:
                    rest = rest + carry
                att = jnp.exp(log_betas[n, p] + rest)
                if first and n == 0:
                    att = jnp.where(causal, att, 0.0)
                row_sum = sums[n, p][:, LANES:]
                carry = row_sum if carry is None else carry + row_sum
                pvs[n, p] = _dot1(att, v_ref[pl.ds(starts[n], bq), sls[p]])
            carry_ref[p] = carry
            m = jnp.max(carry)
            live = m if live is None else jnp.maximum(live, m)
        for p in pairs:
            tot = pvs[0, p]
            for n in range(1, len(js)):
                tot = tot + pvs[n, p]
            if first:
                acc_ref[p] = tot
            else:
                acc_ref[p] += tot
        return live

    def eager(_):
        return jnp.int32(SB_EAGER), process([i - n for n in range(SB_EAGER + 1)], True)

    def diagonal_only(_):
        return jnp.int32(0), process([i], True)

    done, live = lax.cond(i >= SB_EAGER, eager, diagonal_only, None)

    def cond(c):
        return (c[0] < i) & (c[1] > SB_DEAD_LOG)

    def body(c):
        return c[0] + 1, process([i - 1 - c[0]], False)

    lax.while_loop(cond, body, (done, live))
    for p in range(PAIRS):
        sl = slice(p * LANES, (p + 1) * LANES)
        o = jnp.where(head0, acc_ref[p, :bq], acc_ref[p, bq:])
        o_ref[:, sl] = (o * _silu(g_ref[:, sl])).astype(o_ref.dtype)


def _sb(qkv, gate, bsz, tlen):
    nq = tlen // BLOCK_Q
    return pl.pallas_call(
        _sb_kernel,
        grid=(bsz, nq),
        in_specs=[pl.BlockSpec((BLOCK_Q, SB_WIDTH), lambda b, i: (b * nq + i, 0)),
                  pl.BlockSpec((tlen, SB_WIDTH), lambda b, i: (b, 1)),
                  pl.BlockSpec((tlen, SB_WIDTH), lambda b, i: (b, 2)),
                  pl.BlockSpec((BLOCK_Q, SB_WIDTH), lambda b, i: (b * nq + i, 0))],
        out_specs=pl.BlockSpec((BLOCK_Q, SB_WIDTH), lambda b, i: (b * nq + i, 0)),
        out_shape=jax.ShapeDtypeStruct((bsz * tlen, SB_WIDTH), BF16),
        scratch_shapes=[pltpu.VMEM((PAIRS, 2 * BLOCK_Q, LANES), F32),
                        pltpu.VMEM((PAIRS, 2 * BLOCK_Q, LANES), F32)],
        compiler_params=pltpu.CompilerParams(dimension_semantics=("parallel", "arbitrary"),
                                             vmem_limit_bytes=VMEM_LIMIT),
        name="stickbreak",
    )(qkv, qkv, qkv, gate)


def _stack_heads(a, head0):
    return jnp.concatenate([jnp.where(head0, a, 0.0), jnp.where(head0, 0.0, a)], axis=0)


def _rwkv_kernel(*refs, has_vres):
    if has_vres:
        (pr_ref, vf_ref, mu_ref, w0_ref, w2_ref, a0_ref, a2_ref, kks_ref, ka_ref, rk_ref, gng_ref, gnb_ref,
         v0_ref, v1_ref, v2_ref, y_ref, xs_ref, st_ref) = refs
    else:
        (pr_ref, mu_ref, w0_ref, w2_ref, a0_ref, a2_ref, kks_ref, ka_ref, rk_ref, gng_ref, gnb_ref,
         y_ref, vf_out_ref, xs_ref, st_ref) = refs
    c = CHUNK
    w = RWKV_WIDTH
    nb = pr_ref.shape[0]

    @pl.when(pl.program_id(1) == 0)
    def _():
        xs_ref[:, 0:8, :] = jnp.zeros((nb, 8, SHIFT_COLS), F32)
        st_ref[...] = jnp.zeros(st_ref.shape, F32)

    lane = lax.broadcasted_iota(jnp.int32, (c, LANES), 1)
    head0 = lane < HEAD_DIM
    ri = lax.broadcasted_iota(jnp.int32, (LANES, LANES), 0)
    ci = lax.broadcasted_iota(jnp.int32, (LANES, LANES), 1)
    same_head = (ri < HEAD_DIM) == (ci < HEAD_DIM)
    head_ones = jnp.where(same_head, 1.0, 0.0).astype(BF16)
    ti = lax.broadcasted_iota(jnp.int32, (c, LANES), 0)
    ji = lane % HEAD_DIM
    lower = ti > ji
    lower_eq = ti >= ji
    eye = jnp.where(ti == ji, 1.0, 0.0)
    rc = lax.broadcasted_iota(jnp.int32, (c, c), 0)
    cc = lax.broadcasted_iota(jnp.int32, (c, c), 1)
    cum = jnp.where(rc >= cc, 1.0, 0.0).astype(BF16)

    def sequence(b):
        x = pr_ref[b, :, 0:SHIFT_COLS]
        xs_ref[b, 8:8 + c, :] = x
        prev = xs_ref[b, 7:7 + c, :]
        xs_ref[b, 7:8, :] = x[c - 1:c, :]
        f = x + (prev - x) * mu_ref[...]
        r = f[:, 0:w]
        k = f[:, w:2 * w]
        v = f[:, 2 * w:3 * w]
        wa = f[:, 3 * w:3 * w + WA_WIDTH]
        w_pre = w0_ref[...] + _dot1(jnp.tanh(wa), w2_ref[...])
        a_pre = a0_ref[...] + _dot1(wa, a2_ref[...])
        if has_vres:
            v_mid = _dot1(v, v1_ref[...])
        yield
        logw = -jnp.exp(-_softplus(-w_pre) - 0.5)
        a_lr = _sigmoid(a_pre)
        if has_vres:
            v = v + (vf_ref[b] - v) * _sigmoid(v0_ref[...] + _dot1(v_mid, v2_ref[...]))
        else:
            vf_out_ref[b] = v
        lw_hi, lw_lo = _split(logw)
        logp = _mm(cum, lw_hi) + _mm(cum, lw_lo)
        yield
        p = jnp.exp(logp)
        p_prev = jnp.exp(logp - logw)
        inv_p = jnp.exp(-logp)
        kk = k * kks_ref[...]
        k2 = k * (1.0 + (a_lr - 1.0) * ka_ref[...])
        sls = [slice(pr * LANES, (pr + 1) * LANES) for pr in range(PAIRS)]
        n2s = [_mm((kk[:, sl] * kk[:, sl]).astype(BF16), head_ones) for sl in sls]
        yield
        stack = lambda a: _stack_heads(a, head0).astype(BF16)
        chains = []
        for pr, sl in enumerate(sls):
            kk_p = kk[:, sl] / jnp.maximum(jnp.sqrt(n2s[pr]), 1e-12)
            a_n = -kk_p * p_prev[:, sl]
            b_n = kk_p * a_lr[:, sl] * inv_p[:, sl]
            k_n = k2[:, sl] * inv_p[:, sl]
            r_n = r[:, sl] * p[:, sl]
            pe = p[c - 1:c, sl]
            ar = jnp.concatenate([a_n, r_n], axis=0).astype(BF16)
            ch = dict(sl=sl, idx=b * PAIRS + pr, ar=ar, pe=pe, v_st=stack(v[:, sl]),
                      be_ke=jnp.concatenate([(b_n * pe).T, (k_n * pe).T], axis=1).astype(BF16),
                      rkk=(r[:, sl] * k2[:, sl] * rk_ref[:, sl]).astype(BF16), v=v[:, sl])
            ch["gram"] = _mm_nt(ar, jnp.concatenate([stack(b_n), stack(k_n)], axis=0))
            chains.append(ch)
        yield
        for ch in chains:
            gram = ch.pop("gram")
            ab = jnp.where(lower, gram[:c, :LANES], 0.0)
            ak = jnp.where(lower, gram[:c, LANES:], 0.0).astype(BF16)
            ch["rb_rk"] = jnp.concatenate([jnp.where(lower_eq, gram[c:, :LANES], 0.0),
                                           jnp.where(lower_eq, gram[c:, LANES:], 0.0)], axis=1).astype(BF16)
            ch["akv"] = _mm(ak, ch["v_st"])
            g = st_ref[ch["idx"]]
            ch["g"] = g
            ch["ar_g"] = _mm(ch["ar"], g.astype(BF16))
            ch["s"] = eye + ab
            ch["pw"] = _mm(ab.astype(BF16), stack(ab))
        yield
        for _ in range(4):
            for ch in chains:
                res = _mm(ch["pw"].astype(BF16), jnp.concatenate([stack(ch["s"]), stack(ch["pw"])], axis=1))
                ch["s"] = ch["s"] + res[:, :LANES]
                ch["pw"] = res[:, LANES:]
            yield
        for ch in chains:
            ch["s"] = ch["s"] + _mm(ch["pw"].astype(BF16), stack(ch["s"]))
        yield
        for ch in chains:
            xa = ch["ar_g"][:c] + ch["akv"]
            ch["u"] = _mm(ch["s"].astype(BF16), stack(xa))
        yield
        for ch in chains:
            u_st = stack(ch["u"])
            ch["inc"] = _mm(ch["be_ke"], jnp.concatenate([ch["u"].astype(BF16), ch["v"].astype(BF16)], axis=0))
            ch["y_in"] = _mm(ch["rb_rk"], jnp.concatenate([u_st, ch["v_st"]], axis=0))
            ch["bonus"] = _mm(ch["rkk"], head_ones)
        yield
        for ch in chains:
            pcol = jnp.broadcast_to(ch["pe"], (LANES, LANES)).T
            st_ref[ch["idx"]] = pcol * ch["g"] + jnp.where(same_head, ch["inc"], 0.0)
            ch["y"] = ch["ar_g"][c:] + ch["y_in"]
            ch["mean"] = _dot1(ch["y"], head_ones) * (1.0 / HEAD_DIM)
        yield
        for ch in chains:
            d = ch["y"] - ch["mean"]
            ch["d"] = d
            ch["var"] = _dot1(d * d, head_ones) * (1.0 / HEAD_DIM)
        yield
        for ch in chains:
            sl = ch["sl"]
            y = ch["d"] * lax.rsqrt(ch["var"] + GN_EPS) * gng_ref[:, sl] + gnb_ref[:, sl]
            y = y + ch["bonus"] * ch["v"]
            y = y * _silu(pr_ref[b, :, SHIFT_COLS + sl.start:SHIFT_COLS + sl.stop])
            y_ref[b, :, sl] = y.astype(y_ref.dtype)

    programs = [sequence(b) for b in range(nb)]
    while programs:
        programs = [prog for prog in programs if next(prog, True) is None]


def _mm(a, b):
    return jnp.dot(a, b, preferred_element_type=F32)


def _mm_nt(a, b):
    return lax.dot_general(a, b, (((1,), (1,)), ((), ())), preferred_element_type=F32)


def _rwkv(pr, vfirst, mu, w0, w2, a0, a2, kks, ka, rk, gng, gnb, vres, bsz, tlen):
    nc = tlen // CHUNK
    nb = RWKV_SEQS if bsz % RWKV_SEQS == 0 else 1
    has_vres = vres is not None
    full = lambda a: pl.BlockSpec(a.shape, lambda b, t: (0,) * a.ndim)
    row_spec = lambda width: pl.BlockSpec((nb, CHUNK, width), lambda b, t: (b, t, 0))
    params = [mu, w0, w2, a0, a2, kks, ka, rk, gng, gnb]
    in_specs = [row_spec(R_IN)]
    args = [pr]
    if has_vres:
        in_specs.append(row_spec(RWKV_WIDTH))
        args.append(vfirst)
        params = params + list(vres)
    in_specs += [full(a) for a in params]
    args += params
    y_shape = jax.ShapeDtypeStruct((bsz, tlen, RWKV_WIDTH), BF16)
    if has_vres:
        out_specs, out_shape = row_spec(RWKV_WIDTH), y_shape
    else:
        vf_shape = jax.ShapeDtypeStruct((bsz, tlen, RWKV_WIDTH), F32)
        out_specs, out_shape = [row_spec(RWKV_WIDTH), row_spec(RWKV_WIDTH)], [y_shape, vf_shape]
    return pl.pallas_call(
        functools.partial(_rwkv_kernel, has_vres=has_vres),
        grid=(bsz // nb, nc),
        in_specs=in_specs,
        out_specs=out_specs,
        out_shape=out_shape,
        scratch_shapes=[pltpu.VMEM((nb, 8 + CHUNK, SHIFT_COLS), F32),
                        pltpu.VMEM((nb * PAIRS, LANES, LANES), F32)],
        compiler_params=pltpu.CompilerParams(dimension_semantics=("parallel", "arbitrary"),
                                             vmem_limit_bytes=VMEM_LIMIT),
        name="rwkv7_vres" if has_vres else "rwkv7",
    )(*args)


def _outproj_kernel(x_ref, yc_ref, ys_ref, yr_ref, wc_ref, ws_ref, wr_ref, g_ref, o_ref):
    m = _dot1(yc_ref[...], wc_ref[...]) + _dot1(ys_ref[...], ws_ref[...]) + _dot1(yr_ref[...], wr_ref[...])
    ms = jnp.mean(m * m, axis=-1, keepdims=True)
    o_ref[...] = x_ref[...] + m * lax.rsqrt(ms + RMS_EPS) * g_ref[...]


def _outproj(x2, yc, ys, yr, wc, ws, wr, g, tm):
    n = x2.shape[0]
    full = lambda a: pl.BlockSpec(a.shape, lambda i: (0, 0))
    rows = lambda a: pl.BlockSpec((tm, a.shape[1]), lambda i: (i, 0))
    return pl.pallas_call(
        _outproj_kernel,
        grid=(n // tm,),
        in_specs=[rows(x2), rows(yc), rows(ys), rows(yr), full(wc), full(ws), full(wr), full(g)],
        out_specs=rows(x2),
        out_shape=jax.ShapeDtypeStruct(x2.shape, F32),
        compiler_params=pltpu.CompilerParams(dimension_semantics=("parallel",),
                                             vmem_limit_bytes=VMEM_LIMIT),
        name="outproj",
    )(x2, yc, ys, yr, wc, ws, wr, g)


def _pad_rows(a, rows):
    return jnp.concatenate([a, jnp.zeros((rows - a.shape[0],) + a.shape[1:], a.dtype)], axis=0)


def kernel(x, pre_norm_g, post_norm_g, w_in, w_out, conv_dw, conv_dw_b, conv_ln_g, conv_ln_b, conv_pw, conv_pw_b, rwkv_mu, rwkv_w0, rwkv_w2, rwkv_a0, rwkv_a2, rwkv_kk_scale, rwkv_ka, rwkv_rk, rwkv_gn_g, rwkv_gn_b, rwkv_v0, rwkv_v1, rwkv_v2):
    bsz, tlen, dm = x.shape
    depth = w_in.shape[0]
    assert dm == D_MODEL and tlen % BLOCK_Q == 0 and tlen % CHUNK == 0
    n = bsz * tlen
    tm = 256 if tlen % 256 == 0 else BLOCK_Q
    row = lambda a: a.reshape(1, -1)
    c_end = 3 * CONV_WIDTH
    s_end = c_end + 4 * SB_WIDTH
    r_end = s_end + 3 * RWKV_WIDTH
    x2 = x.reshape(n, dm)
    vfirst = None
    for l in range(depth):
        wi = w_in[l]
        wa_cols = jnp.concatenate([wi[:, r_end:r_end + 2 * LORA],
                                   jnp.zeros((dm, WA_WIDTH - 2 * LORA), wi.dtype)], axis=1)
        w_r = jnp.concatenate([wi[:, s_end:r_end], wa_cols, wi[:, r_end + 2 * LORA:]], axis=1).astype(BF16)
        conv_params = (conv_dw[l], row(conv_dw_b[l]), row(conv_ln_g[l]), row(conv_ln_b[l]),
                       conv_pw[l].astype(BF16), row(conv_pw_b[l]))
        yc, qkv, g_sb, pr = _inproj(x2, row(pre_norm_g[l]), wi[:, :c_end].astype(BF16),
                                    wi[:, c_end:s_end].astype(BF16), w_r, conv_params, tm, tlen)
        ys = _sb(qkv, g_sb, bsz, tlen)

        mu = jnp.concatenate([rwkv_mu[l], jnp.zeros((WA_WIDTH - 2 * LORA,), F32)]).reshape(1, -1)
        w2 = _pad_rows(rwkv_w2[l], WA_WIDTH).astype(BF16)
        a2 = _pad_rows(jnp.concatenate([jnp.zeros((LORA, RWKV_WIDTH), F32), rwkv_a2[l]], axis=0),
                       WA_WIDTH).astype(BF16)
        vres = None
        if l > 0:
            v1 = jnp.concatenate([rwkv_v1[l - 1], jnp.zeros((RWKV_WIDTH, LANES - LORA), F32)], axis=1)
            vres = (row(rwkv_v0[l - 1]), v1.astype(BF16), _pad_rows(rwkv_v2[l - 1], LANES).astype(BF16))
        res = _rwkv(pr.reshape(bsz, tlen, R_IN), vfirst, mu, row(rwkv_w0[l]), w2, row(rwkv_a0[l]), a2,
                    row(rwkv_kk_scale[l]), row(rwkv_ka[l]), row(rwkv_rk[l]), row(rwkv_gn_g[l]),
                    row(rwkv_gn_b[l]), vres, bsz, tlen)
        if l == 0:
            yr, vfirst = res
        else:
            yr = res
        yr = yr.reshape(n, RWKV_WIDTH)

        wo = w_out[l].astype(BF16)
        x2 = _outproj(x2, yc, ys, yr, wo[:CONV_WIDTH], wo[CONV_WIDTH:CONV_WIDTH + SB_WIDTH],
                      wo[CONV_WIDTH + SB_WIDTH:], row(post_norm_g[l]), tm)
    return x2.reshape(bsz, tlen, dm)
```

```python
import functools
import math

import jax
import jax.numpy as jnp
from jax import lax
from jax.experimental import pallas as pl
from jax.experimental.pallas import tpu as pltpu

D_MODEL = 1024
HEAD_DIM = 64
CONV_WIDTH = 256
SB_WIDTH = 384
RWKV_WIDTH = 384
CONV_TAPS = 31
LORA = 32
BLOCK_Q = 128
RMS_EPS = 1e-6
LN_EPS = 1e-5
GN_EPS = 64e-5

LANES = 128
SUBLANES = 8
PAIRS = RWKV_WIDTH // LANES
CHUNK = 64
assert CHUNK == HEAD_DIM
RWKV_SEQS = 8
WA_WIDTH = LANES
R_IN = 3 * RWKV_WIDTH + WA_WIDTH + RWKV_WIDTH
SHIFT_COLS = 3 * RWKV_WIDTH + WA_WIDTH
VMEM_LIMIT = 56 * 1024 * 1024

F32 = jnp.float32
BF16 = jnp.bfloat16


def _dot1(a, b):
    return jnp.dot(a.astype(BF16), b.astype(BF16), preferred_element_type=F32)


def _split(a):
    hi = a.astype(BF16)
    lo = (a - hi.astype(F32)).astype(BF16)
    return hi, lo


def _dot_nt1(a, b):
    return lax.dot_general(a.astype(BF16), b.astype(BF16), (((1,), (1,)), ((), ())),
                           preferred_element_type=F32)


def _sigmoid(x):
    return 1.0 / (1.0 + jnp.exp(-x))


def _silu(x):
    return x * _sigmoid(x)


CONV_HALO = 32
CONV_SUB = 64
MXU_COLS = 256


def _proj_slabs(h, w_ref, o_ref, cols=None):
    lo, hi = cols if cols is not None else (0, w_ref.shape[1])

    def slab(c0, c1):
        o_ref[:, c0 - lo:c1 - lo] = jnp.dot(h, w_ref[:, c0:c1], preferred_element_type=F32).astype(o_ref.dtype)

    return [functools.partial(slab, c0, min(c0 + MXU_COLS, hi)) for c0 in range(lo, hi, MXU_COLS)]


def _inproj_kernel(x_ref, g_ref, wc_ref, ws_ref, wr_ref, dw_ref, dwb_ref, lng_ref, lnb_ref, pw_ref, pwb_ref,
                   yc_ref, qkv_ref, gs_ref, pr_ref, pc_ref, u_ref, *, tiles_per_seq):
    tt = x_ref.shape[0]

    @pl.when(pl.program_id(0) % tiles_per_seq == 0)
    def _():
        u_ref[0:CONV_HALO, :] = jnp.zeros((CONV_HALO, CONV_WIDTH), F32)

    x = x_ref[...]
    ms = jnp.mean(x * x, axis=-1, keepdims=True)
    h = (x * lax.rsqrt(ms + RMS_EPS) * g_ref[...]).astype(BF16)
    for job in _proj_slabs(h, wc_ref, pc_ref):
        job()
    u_ref[CONV_HALO:CONV_HALO + tt, :] = pc_ref[:, 0:CONV_WIDTH] * _sigmoid(pc_ref[:, CONV_WIDTH:2 * CONV_WIDTH])
    slabs = (_proj_slabs(h, ws_ref, qkv_ref, (0, 3 * SB_WIDTH))
             + _proj_slabs(h, ws_ref, gs_ref, (3 * SB_WIDTH, 4 * SB_WIDTH)) + _proj_slabs(h, wr_ref, pr_ref))
    groups = tt // CONV_SUB
    for r0 in range(0, tt, CONV_SUB):
        gi = r0 // CONV_SUB
        for job in slabs[gi * len(slabs) // groups:(gi + 1) * len(slabs) // groups]:
            job()
        acc = jnp.broadcast_to(dwb_ref[...], (CONV_SUB, CONV_WIDTH))
        for b in range(SUBLANES):
            part = None
            for a in range((CONV_TAPS - 1 - b) // SUBLANES + 1):
                tap = CONV_TAPS - 1 - (SUBLANES * a + b)
                s = CONV_HALO + r0 - SUBLANES * (a + 1)
                term = dw_ref[tap:tap + 1, :] * u_ref[s:s + CONV_SUB + SUBLANES, :]
                part = term if part is None else part + term
            acc = acc + part[SUBLANES - b:SUBLANES - b + CONV_SUB, :]
        mean = jnp.mean(acc, axis=-1, keepdims=True)
        d = acc - mean
        var = jnp.mean(d * d, axis=-1, keepdims=True)
        y = _silu(d * lax.rsqrt(var + LN_EPS) * lng_ref[...] + lnb_ref[...])
        y = _dot1(y, pw_ref[...]) + pwb_ref[...]
        y = y * _silu(pc_ref[r0:r0 + CONV_SUB, 2 * CONV_WIDTH:3 * CONV_WIDTH])
        yc_ref[r0:r0 + CONV_SUB, :] = y.astype(yc_ref.dtype)
    u_ref[0:CONV_HALO, :] = u_ref[tt:tt + CONV_HALO, :]


def _inproj(x2, g, wc, ws, wr, conv_params, tm, tlen):
    n = x2.shape[0]
    full = lambda a: pl.BlockSpec(a.shape, lambda i: (0, 0))
    rows = lambda width: pl.BlockSpec((tm, width), lambda i: (i, 0))
    widths = (CONV_WIDTH, 3 * SB_WIDTH, SB_WIDTH, wr.shape[1])
    dtypes = (BF16, BF16, F32, F32)
    return pl.pallas_call(
        functools.partial(_inproj_kernel, tiles_per_seq=tlen // tm),
        grid=(n // tm,),
        in_specs=[rows(D_MODEL), full(g), full(wc), full(ws), full(wr)] + [full(a) for a in conv_params],
        out_specs=[rows(wd) for wd in widths],
        out_shape=[jax.ShapeDtypeStruct((n, wd), dt) for wd, dt in zip(widths, dtypes)],
        scratch_shapes=[pltpu.VMEM((tm, wc.shape[1]), F32),
                        pltpu.VMEM((CONV_HALO + tm, CONV_WIDTH), F32)],
        compiler_params=pltpu.CompilerParams(dimension_semantics=("arbitrary",),
                                             vmem_limit_bytes=VMEM_LIMIT),
        name="inproj_conv",
    )(x2, g, wc, ws, wr, *conv_params)


SB_DEAD_LOG = -150.0
SB_EAGER = 2


def _sb_kernel(q_ref, k_ref, v_ref, g_ref, o_ref, carry_ref, acc_ref):
    i = pl.program_id(1)
    bq = BLOCK_Q
    lane = lax.broadcasted_iota(jnp.int32, (bq, LANES), 1)
    head0 = lane < HEAD_DIM
    row = lax.broadcasted_iota(jnp.int32, (LANES, 2 * LANES), 0)
    col = lax.broadcasted_iota(jnp.int32, (LANES, 2 * LANES), 1)
    cs = jnp.where((col >= LANES) | (row > col), -1.0, 0.0).astype(BF16)
    pairs = range(PAIRS)
    sls = [slice(p * LANES, (p + 1) * LANES) for p in pairs]

    def stacked_q(p):
        q = q_ref[:, sls[p]].astype(F32) * (HEAD_DIM ** -0.5)
        return jnp.concatenate([jnp.where(head0, q, 0.0), jnp.where(head0, 0.0, q)], axis=0).astype(BF16)

    def process(js, first):
        starts = [pl.multiple_of(j * bq, bq) for j in js]
        items = [(n, p) for n in range(len(js)) for p in pairs]
        qs = [stacked_q(p) for p in pairs]
        zs = {(n, p): _mm_nt(qs[p], k_ref[pl.ds(starts[n], bq), sls[p]]) for n, p in items}
        if first:
            qi = lax.broadcasted_iota(jnp.int32, (2 * bq, bq), 0) % bq
            ki = lax.broadcasted_iota(jnp.int32, (2 * bq, bq), 1)
            causal = ki < qi
        log_betas, sums = {}, {}
        for n, p in items:
            z = zs[n, p]
            neg_log_keep = jnp.maximum(z, 0.0) + jnp.log(1.0 + jnp.exp(-jnp.abs(z)))
            log_betas[n, p] = z - neg_log_keep
            if first and n == 0:
                neg_log_keep = jnp.where(causal, neg_log_keep, 0.0)
            sums[n, p] = _dot1(neg_log_keep, cs)
        live = None
        pvs = {}
        for p in pairs:
            carry = None if first else carry_ref[p]
            for n in range(len(js)):
                rest = sums[n, p][:, :LANES]
                if carry is not None:
                    rest = rest + carry
                att = jnp.exp(log_betas[n, p] + rest)
                if first and n == 0:
                    att = jnp.where(causal, att, 0.0)
                row_sum = sums[n, p][:, LANES:]
                carry = row_sum if carry is None else carry + row_sum
                pvs[n, p] = _dot1(att, v_ref[pl.ds(starts[n], bq), sls[p]])
            carry_ref[p] = carry
            m = jnp.max(carry)
            live = m if live is None else jnp.maximum(live, m)
        for p in pairs:
            tot = pvs[0, p]
            for n in range(1, len(js)):
                tot = tot + pvs[n, p]
            if first:
                acc_ref[p] = tot
            else:
                acc_ref[p] += tot
        return live

    def eager(_):
        return jnp.int32(SB_EAGER), process([i - n for n in range(SB_EAGER + 1)], True)

    def diagonal_only(_):
        return jnp.int32(0), process([i], True)

    done, live = lax.cond(i >= SB_EAGER, eager, diagonal_only, None)

    def cond(c):
        return (c[0] < i) & (c[1] > SB_DEAD_LOG)

    def body(c):
        return c[0] + 1, process([i - 1 - c[0]], False)

    lax.while_loop(cond, body, (done, live))
    for p in range(PAIRS):
        sl = slice(p * LANES, (p + 1) * LANES)
        o = jnp.where(head0, acc_ref[p, :bq], acc_ref[p, bq:])
        o_ref[:, sl] = (o * _silu(g_ref[:, sl])).astype(o_ref.dtype)


def _sb(qkv, gate, bsz, tlen):
    nq = tlen // BLOCK_Q
    return pl.pallas_call(
        _sb_kernel,
        grid=(bsz, nq),
        in_specs=[pl.BlockSpec((BLOCK_Q, SB_WIDTH), lambda b, i: (b * nq + i, 0)),
                  pl.BlockSpec((tlen, SB_WIDTH), lambda b, i: (b, 1)),
                  pl.BlockSpec((tlen, SB_WIDTH), lambda b, i: (b, 2)),
                  pl.BlockSpec((BLOCK_Q, SB_WIDTH), lambda b, i: (b * nq + i, 0))],
        out_specs=pl.BlockSpec((BLOCK_Q, SB_WIDTH), lambda b, i: (b * nq + i, 0)),
        out_shape=jax.ShapeDtypeStruct((bsz * tlen, SB_WIDTH), BF16),
        scratch_shapes=[pltpu.VMEM((PAIRS, 2 * BLOCK_Q, LANES), F32),
                        pltpu.VMEM((PAIRS, 2 * BLOCK_Q, LANES), F32)],
        compiler_params=pltpu.CompilerParams(dimension_semantics=("parallel", "arbitrary"),
                                             vmem_limit_bytes=VMEM_LIMIT),
        name="stickbreak",
    )(qkv, qkv, qkv, gate)


def _stack_heads(a, head0):
    return jnp.concatenate([jnp.where(head0, a, 0.0), jnp.where(head0, 0.0, a)], axis=0)


def _rwkv_kernel(*refs, has_vres):
    if has_vres:
        (pr_ref, vf_ref, mu_ref, w0_ref, w2_ref, a0_ref, a2_ref, kks_ref, ka_ref, rk_ref, gng_ref, gnb_ref,
         v0_ref, v1_ref, v2_ref, y_ref, xs_ref, st_ref) = refs
    else:
        (pr_ref, mu_ref, w0_ref, w2_ref, a0_ref, a2_ref, kks_ref, ka_ref, rk_ref, gng_ref, gnb_ref,
         y_ref, vf_out_ref, xs_ref, st_ref) = refs
    c = CHUNK
    w = RWKV_WIDTH
    nb = pr_ref.shape[0]

    @pl.when(pl.program_id(1) == 0)
    def _():
        xs_ref[...] = jnp.zeros(xs_ref.shape, F32)
        st_ref[...] = jnp.zeros(st_ref.shape, F32)

    lane = lax.broadcasted_iota(jnp.int32, (c, LANES), 1)
    head0 = lane < HEAD_DIM
    ri = lax.broadcasted_iota(jnp.int32, (LANES, LANES), 0)
    ci = lax.broadcasted_iota(jnp.int32, (LANES, LANES), 1)
    same_head = (ri < HEAD_DIM) == (ci < HEAD_DIM)
    head_ones = jnp.where(same_head, 1.0, 0.0).astype(BF16)
    ti = lax.broadcasted_iota(jnp.int32, (c, LANES), 0)
    ji = lane % HEAD_DIM
    lower = ti > ji
    lower_eq = ti >= ji
    eye = jnp.where(ti == ji, 1.0, 0.0)
    rc = lax.broadcasted_iota(jnp.int32, (c, c), 0)
    cc = lax.broadcasted_iota(jnp.int32, (c, c), 1)
    cum = jnp.where(rc >= cc, 1.0, 0.0).astype(BF16)

    def sequence(b):
        x = pr_ref[b, :, 0:SHIFT_COLS]
        first_row = lax.broadcasted_iota(jnp.int32, (c, SHIFT_COLS), 0) == 0
        prev = jnp.where(first_row, xs_ref[b, 0:1, :], pltpu.roll(x, 1, axis=0))
        xs_ref[b, 0:1, :] = x[c - 1:c, :]
        f = x + (prev - x) * mu_ref[...]
        r = f[:, 0:w]
        k = f[:, w:2 * w]
        v = f[:, 2 * w:3 * w]
        wa = f[:, 3 * w:3 * w + WA_WIDTH]
        w_pre = w0_ref[...] + _dot1(jnp.tanh(wa), w2_ref[...])
        a_pre = a0_ref[...] + _dot1(wa, a2_ref[...])
        if has_vres:
            v_mid = _dot1(v, v1_ref[...])
        yield
        logw = (-math.exp(-0.5)) * _sigmoid(w_pre)
        a_lr = _sigmoid(a_pre)
        if has_vres:
            v = v + (vf_ref[b] - v) * _sigmoid(v0_ref[...] + _dot1(v_mid, v2_ref[...]))
        else:
            vf_out_ref[b] = v
        lw_hi, lw_lo = _split(logw)
        logp = _mm(cum, lw_hi) + _mm(cum, lw_lo)
        yield
        p = jnp.exp(logp)
        p_prev = jnp.exp(logp - logw)
        inv_p = jnp.exp(-logp)
        kk = k * kks_ref[...]
        k2 = k * (1.0 + (a_lr - 1.0) * ka_ref[...])
        sls = [slice(pr * LANES, (pr + 1) * LANES) for pr in range(PAIRS)]
        n2s = [_mm((kk[:, sl] * kk[:, sl]).astype(BF16), head_ones) for sl in sls]
        yield
        stack = lambda a: _stack_heads(a, head0).astype(BF16)
        chains = []
        for pr, sl in enumerate(sls):
            kk_p = kk[:, sl] * lax.rsqrt(jnp.maximum(n2s[pr], 1e-24))
            a_n = -kk_p * p_prev[:, sl]
            b_n = kk_p * a_lr[:, sl] * inv_p[:, sl]
            k_n = k2[:, sl] * inv_p[:, sl]
            r_n = r[:, sl] * p[:, sl]
            pe = p[c - 1:c, sl]
            ar = jnp.concatenate([a_n, r_n], axis=0).astype(BF16)
            ch = dict(sl=sl, idx=b * PAIRS + pr, ar=ar, pe=pe, v_st=stack(v[:, sl]),
                      be_ke=jnp.concatenate([(b_n * pe).T, (k_n * pe).T], axis=1).astype(BF16),
                      rkk=(r[:, sl] * k2[:, sl] * rk_ref[:, sl]).astype(BF16), v=v[:, sl])
            ch["gram"] = _mm_nt(ar, jnp.concatenate([stack(b_n), stack(k_n)], axis=0))
            chains.append(ch)
        yield
        for ch in chains:
            gram = ch.pop("gram")
            ab = jnp.where(lower, gram[:c, :LANES], 0.0)
            ak = jnp.where(lower, gram[:c, LANES:], 0.0).astype(BF16)
            ch["rb_rk"] = jnp.concatenate([jnp.where(lower_eq, gram[c:, :LANES], 0.0),
                                           jnp.where(lower_eq, gram[c:, LANES:], 0.0)], axis=1).astype(BF16)
            ch["akv"] = _mm(ak, ch["v_st"])
            g = st_ref[ch["idx"]]
            ch["g"] = g
            ch["ar_g"] = _mm(ch["ar"], g.astype(BF16))
            ch["s"] = eye + ab
            ch["pw"] = _mm(ab.astype(BF16), stack(ab))
        yield
        for _ in range(4):
            for ch in chains:
                res = _mm(ch["pw"].astype(BF16), jnp.concatenate([stack(ch["s"]), stack(ch["pw"])], axis=1))
                ch["s"] = ch["s"] + res[:, :LANES]
                ch["pw"] = res[:, LANES:]
            yield
        for ch in chains:
            ch["s"] = ch["s"] + _mm(ch["pw"].astype(BF16), stack(ch["s"]))
        yield
        for ch in chains:
            xa = ch["ar_g"][:c] + ch["akv"]
            ch["u"] = _mm(ch["s"].astype(BF16), stack(xa))
        yield
        for ch in chains:
            u_st = stack(ch["u"])
            ch["inc"] = _mm(ch["be_ke"], jnp.concatenate([ch["u"].astype(BF16), ch["v"].astype(BF16)], axis=0))
            ch["y_in"] = _mm(ch["rb_rk"], jnp.concatenate([u_st, ch["v_st"]], axis=0))
            ch["bonus"] = _mm(ch["rkk"], head_ones)
        yield
        for ch in chains:
            pcol = jnp.broadcast_to(ch["pe"], (LANES, LANES)).T
            st_ref[ch["idx"]] = pcol * ch["g"] + jnp.where(same_head, ch["inc"], 0.0)
            ch["y"] = ch["ar_g"][c:] + ch["y_in"]
            ch["mean"] = _dot1(ch["y"], head_ones) * (1.0 / HEAD_DIM)
        yield
        for ch in chains:
            d = ch["y"] - ch["mean"]
            ch["d"] = d
            ch["var"] = _dot1(d * d, head_ones) * (1.0 / HEAD_DIM)
        yield
        for ch in chains:
            sl = ch["sl"]
            y = ch["d"] * lax.rsqrt(ch["var"] + GN_EPS) * gng_ref[:, sl] + gnb_ref[:, sl]
            y = y + ch["bonus"] * ch["v"]
            y = y * _silu(pr_ref[b, :, SHIFT_COLS + sl.start:SHIFT_COLS + sl.stop])
            y_ref[b, :, sl] = y.astype(y_ref.dtype)

    programs = [sequence(b) for b in range(nb)]
    while programs:
        programs = [prog for prog in programs if next(prog, True) is None]


def _mm(a, b):
    return jnp.dot(a, b, preferred_element_type=F32)


def _mm_nt(a, b):
    return lax.dot_general(a, b, (((1,), (1,)), ((), ())), preferred_element_type=F32)


def _rwkv(pr, vfirst, mu, w0, w2, a0, a2, kks, ka, rk, gng, gnb, vres, bsz, tlen):
    nc = tlen // CHUNK
    nb = RWKV_SEQS if bsz % RWKV_SEQS == 0 else 1
    has_vres = vres is not None
    full = lambda a: pl.BlockSpec(a.shape, lambda b, t: (0,) * a.ndim)
    row_spec = lambda width: pl.BlockSpec((nb, CHUNK, width), lambda b, t: (b, t, 0))
    params = [mu, w0, w2, a0, a2, kks, ka, rk, gng, gnb]
    in_specs = [row_spec(R_IN)]
    args = [pr]
    if has_vres:
        in_specs.append(row_spec(RWKV_WIDTH))
        args.append(vfirst)
        params = params + list(vres)
    in_specs += [full(a) for a in params]
    args += params
    y_shape = jax.ShapeDtypeStruct((bsz, tlen, RWKV_WIDTH), BF16)
    if has_vres:
        out_specs, out_shape = row_spec(RWKV_WIDTH), y_shape
    else:
        vf_shape = jax.ShapeDtypeStruct((bsz, tlen, RWKV_WIDTH), F32)
        out_specs, out_shape = [row_spec(RWKV_WIDTH), row_spec(RWKV_WIDTH)], [y_shape, vf_shape]
    return pl.pallas_call(
        functools.partial(_rwkv_kernel, has_vres=has_vres),
        grid=(bsz // nb, nc),
        in_specs=in_specs,
        out_specs=out_specs,
        out_shape=out_shape,
        scratch_shapes=[pltpu.VMEM((nb, SUBLANES, SHIFT_COLS), F32),
                        pltpu.VMEM((nb * PAIRS, LANES, LANES), F32)],
        compiler_params=pltpu.CompilerParams(dimension_semantics=("parallel", "arbitrary"),
                                             vmem_limit_bytes=VMEM_LIMIT),
        name="rwkv7_vres" if has_vres else "rwkv7",
    )(*args)


OUT_SUB = 256


def _outproj_kernel(x_ref, yc_ref, ys_ref, yr_ref, w_ref, g_ref, o_ref):
    groups = [slice(r0, r0 + OUT_SUB) for r0 in range(0, x_ref.shape[0], OUT_SUB)]
    mixes = [jnp.dot(jnp.concatenate([yc_ref[rs, :], ys_ref[rs, :], yr_ref[rs, :]], axis=1), w_ref[...],
                     preferred_element_type=F32) for rs in groups]
    for rs, m in zip(groups, mixes):
        ms = jnp.mean(m * m, axis=-1, keepdims=True)
        o_ref[rs, :] = x_ref[rs, :] + m * lax.rsqrt(ms + RMS_EPS) * g_ref[...]


def _outproj(x2, yc, ys, yr, w, g, tm):
    n = x2.shape[0]
    full = lambda a: pl.BlockSpec(a.shape, lambda i: (0, 0))
    rows = lambda a: pl.BlockSpec((tm, a.shape[1]), lambda i: (i, 0))
    return pl.pallas_call(
        _outproj_kernel,
        grid=(n // tm,),
        in_specs=[rows(x2), rows(yc), rows(ys), rows(yr), full(w), full(g)],
        out_specs=rows(x2),
        out_shape=jax.ShapeDtypeStruct(x2.shape, F32),
        compiler_params=pltpu.CompilerParams(dimension_semantics=("parallel",),
                                             vmem_limit_bytes=VMEM_LIMIT),
        name="outproj",
    )(x2, yc, ys, yr, w, g)


def _pad_rows(a, rows):
    return jnp.concatenate([a, jnp.zeros((rows - a.shape[0],) + a.shape[1:], a.dtype)], axis=0)


def kernel(x, pre_norm_g, post_norm_g, w_in, w_out, conv_dw, conv_dw_b, conv_ln_g, conv_ln_b, conv_pw, conv_pw_b, rwkv_mu, rwkv_w0, rwkv_w2, rwkv_a0, rwkv_a2, rwkv_kk_scale, rwkv_ka, rwkv_rk, rwkv_gn_g, rwkv_gn_b, rwkv_v0, rwkv_v1, rwkv_v2):
    bsz, tlen, dm = x.shape
    depth = w_in.shape[0]
    assert dm == D_MODEL and tlen % BLOCK_Q == 0 and tlen % CHUNK == 0
    n = bsz * tlen
    tm = 256 if tlen % 256 == 0 else BLOCK_Q
    tm_out = 2 * OUT_SUB if n % (2 * OUT_SUB) == 0 else OUT_SUB
    assert n % tm_out == 0
    row = lambda a: a.reshape(1, -1)
    c_end = 3 * CONV_WIDTH
    s_end = c_end + 4 * SB_WIDTH
    r_end = s_end + 3 * RWKV_WIDTH
    x2 = x.reshape(n, dm)
    vfirst = None
    for l in range(depth):
        wi = w_in[l]
        wa_cols = jnp.concatenate([wi[:, r_end:r_end + 2 * LORA],
                                   jnp.zeros((dm, WA_WIDTH - 2 * LORA), wi.dtype)], axis=1)
        w_r = jnp.concatenate([wi[:, s_end:r_end], wa_cols, wi[:, r_end + 2 * LORA:]], axis=1).astype(BF16)
        conv_params = (conv_dw[l], row(conv_dw_b[l]), row(conv_ln_g[l]), row(conv_ln_b[l]),
                       conv_pw[l].astype(BF16), row(conv_pw_b[l]))
        yc, qkv, g_sb, pr = _inproj(x2, row(pre_norm_g[l]), wi[:, :c_end].astype(BF16),
                                    wi[:, c_end:s_end].astype(BF16), w_r, conv_params, tm, tlen)
        ys = _sb(qkv, g_sb, bsz, tlen)

        mu = jnp.concatenate([rwkv_mu[l], jnp.zeros((WA_WIDTH - 2 * LORA,), F32)]).reshape(1, -1)
        w2 = _pad_rows(rwkv_w2[l], WA_WIDTH).astype(BF16)
        a2 = _pad_rows(jnp.concatenate([jnp.zeros((LORA, RWKV_WIDTH), F32), rwkv_a2[l]], axis=0),
                       WA_WIDTH).astype(BF16)
        vres = None
        if l > 0:
            v1 = jnp.concatenate([rwkv_v1[l - 1], jnp.zeros((RWKV_WIDTH, LANES - LORA), F32)], axis=1)
            vres = (row(rwkv_v0[l - 1]), v1.astype(BF16), _pad_rows(rwkv_v2[l - 1], LANES).astype(BF16))
        res = _rwkv(pr.reshape(bsz, tlen, R_IN), vfirst, mu, row(rwkv_w0[l]), w2, row(rwkv_a0[l]), a2,
                    row(rwkv_kk_scale[l]), row(rwkv_ka[l]), row(rwkv_rk[l]), row(rwkv_gn_g[l]),
                    row(rwkv_gn_b[l]), vres, bsz, tlen)
        if l == 0:
            yr, vfirst = res
        else:
            yr = res
        yr = yr.reshape(n, RWKV_WIDTH)

        x2 = _outproj(x2, yc, ys, yr, w_out[l].astype(BF16), row(post_norm_g[l]), tm_out)
    return x2.reshape(bsz, tlen, dm)
```

```python
import functools
import math

import jax
import jax.numpy as jnp
from jax import lax
from jax.experimental import pallas as pl
from jax.experimental.pallas import tpu as pltpu

D_MODEL = 1024
HEAD_DIM = 64
CONV_WIDTH = 256
SB_WIDTH = 384
RWKV_WIDTH = 384
CONV_TAPS = 31
LORA = 32
BLOCK_Q = 128
RMS_EPS = 1e-6
LN_EPS = 1e-5
GN_EPS = 64e-5

LANES = 128
SUBLANES = 8
PAIRS = RWKV_WIDTH // LANES
CHUNK = 64
assert CHUNK == HEAD_DIM
RWKV_SEQS = 8
WA_WIDTH = LANES
R_IN = 3 * RWKV_WIDTH + WA_WIDTH + RWKV_WIDTH
SHIFT_COLS = 3 * RWKV_WIDTH + WA_WIDTH
VMEM_LIMIT = 56 * 1024 * 1024

F32 = jnp.float32
BF16 = jnp.bfloat16


def _dot1(a, b):
    return jnp.dot(a.astype(BF16), b.astype(BF16), preferred_element_type=F32)


def _split(a):
    hi = a.astype(BF16)
    lo = (a - hi.astype(F32)).astype(BF16)
    return hi, lo


def _dot_nt1(a, b):
    return lax.dot_general(a.astype(BF16), b.astype(BF16), (((1,), (1,)), ((), ())),
                           preferred_element_type=F32)


def _sigmoid(x):
    return 1.0 / (1.0 + jnp.exp(-x))


def _silu(x):
    return x * _sigmoid(x)


CONV_HALO = 32
CONV_SUB = 64
MXU_COLS = 256


def _proj_slabs(h, w_ref, o_ref, cols=None):
    lo, hi = cols if cols is not None else (0, w_ref.shape[1])

    def slab(c0, c1):
        o_ref[:, c0 - lo:c1 - lo] = jnp.dot(h, w_ref[:, c0:c1], preferred_element_type=F32).astype(o_ref.dtype)

    return [functools.partial(slab, c0, min(c0 + MXU_COLS, hi)) for c0 in range(lo, hi, MXU_COLS)]


def _inproj_kernel(x_ref, g_ref, wc_ref, ws_ref, wr_ref, dw_ref, dwb_ref, lng_ref, lnb_ref, pw_ref, pwb_ref,
                   yc_ref, qkv_ref, gs_ref, pr_ref, pc_ref, u_ref, *, tiles_per_seq):
    tt = x_ref.shape[0]

    @pl.when(pl.program_id(0) % tiles_per_seq == 0)
    def _():
        u_ref[0:CONV_HALO, :] = jnp.zeros((CONV_HALO, CONV_WIDTH), F32)

    x = x_ref[...]
    ms = jnp.mean(x * x, axis=-1, keepdims=True)
    h = (x * lax.rsqrt(ms + RMS_EPS) * g_ref[...]).astype(BF16)
    for job in _proj_slabs(h, wc_ref, pc_ref):
        job()
    u_ref[CONV_HALO:CONV_HALO + tt, :] = pc_ref[:, 0:CONV_WIDTH] * _sigmoid(pc_ref[:, CONV_WIDTH:2 * CONV_WIDTH])
    slabs = (_proj_slabs(h, ws_ref, qkv_ref, (0, 3 * SB_WIDTH))
             + _proj_slabs(h, ws_ref, gs_ref, (3 * SB_WIDTH, 4 * SB_WIDTH)) + _proj_slabs(h, wr_ref, pr_ref))
    groups = tt // CONV_SUB
    for r0 in range(0, tt, CONV_SUB):
        gi = r0 // CONV_SUB
        for job in slabs[gi * len(slabs) // groups:(gi + 1) * len(slabs) // groups]:
            job()
        halves = []
        for l0 in range(0, CONV_WIDTH, LANES):
            ls = slice(l0, l0 + LANES)
            window = u_ref[r0:r0 + CONV_HALO + CONV_SUB, ls]
            acc = jnp.broadcast_to(dwb_ref[:, ls], (CONV_SUB, LANES))
            for b in range(SUBLANES):
                part = None
                for a in range((CONV_TAPS - 1 - b) // SUBLANES + 1):
                    tap = CONV_TAPS - 1 - (SUBLANES * a + b)
                    s = CONV_HALO - SUBLANES * (a + 1)
                    term = dw_ref[tap:tap + 1, ls] * window[s:s + CONV_SUB + SUBLANES, :]
                    part = term if part is None else part + term
                acc = acc + part[SUBLANES - b:SUBLANES - b + CONV_SUB, :]
            halves.append(acc)
        acc = jnp.concatenate(halves, axis=1)
        mean = jnp.mean(acc, axis=-1, keepdims=True)
        d = acc - mean
        var = jnp.mean(d * d, axis=-1, keepdims=True)
        y = _silu(d * lax.rsqrt(var + LN_EPS) * lng_ref[...] + lnb_ref[...])
        y = _dot1(y, pw_ref[...]) + pwb_ref[...]
        y = y * _silu(pc_ref[r0:r0 + CONV_SUB, 2 * CONV_WIDTH:3 * CONV_WIDTH])
        yc_ref[r0:r0 + CONV_SUB, :] = y.astype(yc_ref.dtype)
    u_ref[0:CONV_HALO, :] = u_ref[tt:tt + CONV_HALO, :]


def _inproj(x2, g, wc, ws, wr, conv_params, tm, tlen):
    n = x2.shape[0]
    full = lambda a: pl.BlockSpec(a.shape, lambda i: (0, 0))
    rows = lambda width: pl.BlockSpec((tm, width), lambda i: (i, 0))
    widths = (CONV_WIDTH, 3 * SB_WIDTH, SB_WIDTH, wr.shape[1])
    dtypes = (BF16, BF16, F32, F32)
    return pl.pallas_call(
        functools.partial(_inproj_kernel, tiles_per_seq=tlen // tm),
        grid=(n // tm,),
        in_specs=[rows(D_MODEL), full(g), full(wc), full(ws), full(wr)] + [full(a) for a in conv_params],
        out_specs=[rows(wd) for wd in widths],
        out_shape=[jax.ShapeDtypeStruct((n, wd), dt) for wd, dt in zip(widths, dtypes)],
        scratch_shapes=[pltpu.VMEM((tm, wc.shape[1]), F32),
                        pltpu.VMEM((CONV_HALO + tm, CONV_WIDTH), F32)],
        compiler_params=pltpu.CompilerParams(dimension_semantics=("arbitrary",),
                                             vmem_limit_bytes=VMEM_LIMIT),
        name="inproj_conv",
    )(x2, g, wc, ws, wr, *conv_params)


SB_DEAD_LOG = -150.0
SB_EAGER = 2


def _sb_kernel(q_ref, k_ref, v_ref, g_ref, o_ref, carry_ref, acc_ref):
    i = pl.program_id(1)
    bq = BLOCK_Q
    lane = lax.broadcasted_iota(jnp.int32, (bq, LANES), 1)
    head0 = lane < HEAD_DIM
    row = lax.broadcasted_iota(jnp.int32, (LANES, 2 * LANES), 0)
    col = lax.broadcasted_iota(jnp.int32, (LANES, 2 * LANES), 1)
    cs = jnp.where((col >= LANES) | (row > col), -1.0, 0.0).astype(BF16)
    pairs = range(PAIRS)
    sls = [slice(p * LANES, (p + 1) * LANES) for p in pairs]

    def stacked_q(p):
        q = q_ref[:, sls[p]].astype(F32) * (HEAD_DIM ** -0.5)
        return jnp.concatenate([jnp.where(head0, q, 0.0), jnp.where(head0, 0.0, q)], axis=0).astype(BF16)

    def process(js, first):
        starts = [pl.multiple_of(j * bq, bq) for j in js]
        items = [(n, p) for n in range(len(js)) for p in pairs]
        qs = [stacked_q(p) for p in pairs]
        zs = {(n, p): _mm_nt(qs[p], k_ref[pl.ds(starts[n], bq), sls[p]]) for n, p in items}
        if first:
            qi = lax.broadcasted_iota(jnp.int32, (2 * bq, bq), 0) % bq
            ki = lax.broadcasted_iota(jnp.int32, (2 * bq, bq), 1)
            causal = ki < qi
        log_betas, sums = {}, {}
        for n, p in items:
            z = zs[n, p]
            neg_log_keep = jnp.maximum(z, 0.0) + jnp.log(1.0 + jnp.exp(-jnp.abs(z)))
            log_betas[n, p] = z - neg_log_keep
            if first and n == 0:
                neg_log_keep = jnp.where(causal, neg_log_keep, 0.0)
            sums[n, p] = _dot1(neg_log_keep, cs)
        live = None
        pvs = {}
        for p in pairs:
            carry = None if first else carry_ref[p]
            for n in range(len(js)):
                rest = sums[n, p][:, :LANES]
                if carry is not None:
                    rest = rest + carry
                att = jnp.exp(log_betas[n, p] + rest)
                if first and n == 0:
                    att = jnp.where(causal, att, 0.0)
                row_sum = sums[n, p][:, LANES:]
                carry = row_sum if carry is None else carry + row_sum
                pvs[n, p] = _dot1(att, v_ref[pl.ds(starts[n], bq), sls[p]])
            carry_ref[p] = carry
            m = jnp.max(carry)
            live = m if live is None else jnp.maximum(live, m)
        for p in pairs:
            tot = pvs[0, p]
            for n in range(1, len(js)):
                tot = tot + pvs[n, p]
            if first:
                acc_ref[p] = tot
            else:
                acc_ref[p] += tot
        return live

    def eager(_):
        return jnp.int32(SB_EAGER), process([i - n for n in range(SB_EAGER + 1)], True)

    def diagonal_only(_):
        return jnp.int32(0), process([i], True)

    done, live = lax.cond(i >= SB_EAGER, eager, diagonal_only, None)

    def cond(c):
        return (c[0] < i) & (c[1] > SB_DEAD_LOG)

    def body(c):
        return c[0] + 1, process([i - 1 - c[0]], False)

    lax.while_loop(cond, body, (done, live))
    for p in range(PAIRS):
        sl = slice(p * LANES, (p + 1) * LANES)
        o = jnp.where(head0, acc_ref[p, :bq], acc_ref[p, bq:])
        o_ref[:, sl] = (o * _silu(g_ref[:, sl])).astype(o_ref.dtype)


def _sb(qkv, gate, bsz, tlen):
    nq = tlen // BLOCK_Q
    return pl.pallas_call(
        _sb_kernel,
        grid=(bsz, nq),
        in_specs=[pl.BlockSpec((BLOCK_Q, SB_WIDTH), lambda b, i: (b * nq + i, 0)),
                  pl.BlockSpec((tlen, SB_WIDTH), lambda b, i: (b, 1)),
                  pl.BlockSpec((tlen, SB_WIDTH), lambda b, i: (b, 2)),
                  pl.BlockSpec((BLOCK_Q, SB_WIDTH), lambda b, i: (b * nq + i, 0))],
        out_specs=pl.BlockSpec((BLOCK_Q, SB_WIDTH), lambda b, i: (b * nq + i, 0)),
        out_shape=jax.ShapeDtypeStruct((bsz * tlen, SB_WIDTH), BF16),
        scratch_shapes=[pltpu.VMEM((PAIRS, 2 * BLOCK_Q, LANES), F32),
                        pltpu.VMEM((PAIRS, 2 * BLOCK_Q, LANES), F32)],
        compiler_params=pltpu.CompilerParams(dimension_semantics=("parallel", "arbitrary"),
                                             vmem_limit_bytes=VMEM_LIMIT),
        name="stickbreak",
    )(qkv, qkv, qkv, gate)


def _stack_heads(a, head0):
    return jnp.concatenate([jnp.where(head0, a, 0.0), jnp.where(head0, 0.0, a)], axis=0)


def _rwkv_kernel(*refs, has_vres):
    if has_vres:
        (pr_ref, vf_ref, mu_ref, w0_ref, w2_ref, a0_ref, a2_ref, kks_ref, ka_ref, rk_ref, gng_ref, gnb_ref,
         v0_ref, v1_ref, v2_ref, y_ref, xs_ref, st_ref) = refs
    else:
        (pr_ref, mu_ref, w0_ref, w2_ref, a0_ref, a2_ref, kks_ref, ka_ref, rk_ref, gng_ref, gnb_ref,
         y_ref, vf_out_ref, xs_ref, st_ref) = refs
    c = CHUNK
    w = RWKV_WIDTH
    nb = pr_ref.shape[0]

    @pl.when(pl.program_id(1) == 0)
    def _():
        xs_ref[...] = jnp.zeros(xs_ref.shape, F32)
        st_ref[...] = jnp.zeros(st_ref.shape, F32)

    lane = lax.broadcasted_iota(jnp.int32, (c, LANES), 1)
    head0 = lane < HEAD_DIM
    ri = lax.broadcasted_iota(jnp.int32, (LANES, LANES), 0)
    ci = lax.broadcasted_iota(jnp.int32, (LANES, LANES), 1)
    same_head = (ri < HEAD_DIM) == (ci < HEAD_DIM)
    head_ones = jnp.where(same_head, 1.0, 0.0).astype(BF16)
    weights = dict(w2=w2_ref[...], a2=a2_ref[...])
    if has_vres:
        weights.update(v1=v1_ref[...], v2=v2_ref[...])
    ti = lax.broadcasted_iota(jnp.int32, (c, LANES), 0)
    ji = lane % HEAD_DIM
    lower = ti > ji
    lower_eq = ti >= ji
    eye = jnp.where(ti == ji, 1.0, 0.0)
    rc = lax.broadcasted_iota(jnp.int32, (c, c), 0)
    cc = lax.broadcasted_iota(jnp.int32, (c, c), 1)
    cum = jnp.where(rc >= cc, 1.0, 0.0).astype(BF16)

    def sequence(b):
        x = pr_ref[b, :, 0:SHIFT_COLS]
        first_row = lax.broadcasted_iota(jnp.int32, (c, SHIFT_COLS), 0) == 0
        prev = jnp.where(first_row, xs_ref[b, 0:1, :], pltpu.roll(x, 1, axis=0))
        xs_ref[b, 0:1, :] = x[c - 1:c, :]
        f = x + (prev - x) * mu_ref[...]
        r = f[:, 0:w]
        k = f[:, w:2 * w]
        v = f[:, 2 * w:3 * w]
        wa = f[:, 3 * w:3 * w + WA_WIDTH]
        lora = [(jnp.tanh(wa).astype(BF16), weights["w2"]), (wa.astype(BF16), weights["a2"])]
        if has_vres:
            lora.append((v.astype(BF16), weights["v1"]))
        products = yield lora
        logw = (-math.exp(-0.5)) * _sigmoid(w0_ref[...] + products[0])
        a_lr = _sigmoid(a0_ref[...] + products[1])
        lw_hi, lw_lo = _split(logw)
        logp = _mm(cum, lw_hi) + _mm(cum, lw_lo)
        if has_vres:
            (v_up,) = yield [(products[2].astype(BF16), weights["v2"])]
            v = v + (vf_ref[b] - v) * _sigmoid(v0_ref[...] + v_up)
        else:
            vf_out_ref[b] = v
            yield []
        p = jnp.exp(logp)
        p_prev = jnp.exp(logp - logw)
        inv_p = jnp.exp(-logp)
        kk = k * kks_ref[...]
        k2 = k * (1.0 + (a_lr - 1.0) * ka_ref[...])
        sls = [slice(pr * LANES, (pr + 1) * LANES) for pr in range(PAIRS)]
        n2s = yield [((kk[:, sl] * kk[:, sl]).astype(BF16), head_ones) for sl in sls]
        stack = lambda a: _stack_heads(a, head0).astype(BF16)
        chains = []
        for pr, sl in enumerate(sls):
            kk_p = kk[:, sl] * lax.rsqrt(jnp.maximum(n2s[pr], 1e-24))
            a_n = -kk_p * p_prev[:, sl]
            b_n = kk_p * a_lr[:, sl] * inv_p[:, sl]
            k_n = k2[:, sl] * inv_p[:, sl]
            r_n = r[:, sl] * p[:, sl]
            pe = p[c - 1:c, sl]
            ar = jnp.concatenate([a_n, r_n], axis=0).astype(BF16)
            ch = dict(sl=sl, idx=b * PAIRS + pr, ar=ar, pe=pe, v_st=stack(v[:, sl]),
                      be_ke=jnp.concatenate([(b_n * pe).T, (k_n * pe).T], axis=1).astype(BF16),
                      rkk=(r[:, sl] * k2[:, sl] * rk_ref[:, sl]).astype(BF16), v=v[:, sl])
            ch["gram"] = _mm_nt(ar, jnp.concatenate([stack(b_n), stack(k_n)], axis=0))
            chains.append(ch)
        yield []
        requests = []
        for ch in chains:
            gram = ch.pop("gram")
            ab = jnp.where(lower, gram[:c, :LANES], 0.0)
            ak = jnp.where(lower, gram[:c, LANES:], 0.0).astype(BF16)
            ch["rb_rk"] = jnp.concatenate([jnp.where(lower_eq, gram[c:, :LANES], 0.0),
                                           jnp.where(lower_eq, gram[c:, LANES:], 0.0)], axis=1).astype(BF16)
            g = st_ref[ch["idx"]]
            ch["g"] = g
            ch["s"] = eye + ab
            requests += [(ak, ch["v_st"]), (ch["ar"], g.astype(BF16)), (ab.astype(BF16), stack(ab))]
        products = yield requests
        for i, ch in enumerate(chains):
            ch["akv"], ch["ar_g"], ch["pw"] = products[3 * i:3 * i + 3]
        for _ in range(4):
            for ch in chains:
                res = _mm(ch["pw"].astype(BF16), jnp.concatenate([stack(ch["s"]), stack(ch["pw"])], axis=1))
                ch["s"] = ch["s"] + res[:, :LANES]
                ch["pw"] = res[:, LANES:]
            yield []
        products = yield [(ch["pw"].astype(BF16), stack(ch["s"])) for ch in chains]
        for ch, prod in zip(chains, products):
            ch["s"] = ch["s"] + prod
        products = yield [(ch["s"].astype(BF16), stack(ch["ar_g"][:c] + ch["akv"])) for ch in chains]
        requests = []
        for ch, u in zip(chains, products):
            ch["y_in"] = _mm(ch["rb_rk"], jnp.concatenate([stack(u), ch["v_st"]], axis=0))
            requests += [(ch["be_ke"], jnp.concatenate([u.astype(BF16), ch["v"].astype(BF16)], axis=0)),
                         (ch["rkk"], head_ones)]
        products = yield requests
        requests = []
        for i, ch in enumerate(chains):
            inc, ch["bonus"] = products[2 * i:2 * i + 2]
            pcol = jnp.broadcast_to(ch["pe"], (LANES, LANES)).T
            st_ref[ch["idx"]] = pcol * ch["g"] + jnp.where(same_head, inc, 0.0)
            ch["y"] = ch["ar_g"][c:] + ch["y_in"]
            requests.append((ch["y"].astype(BF16), head_ones))
        means = yield requests
        for ch, mean in zip(chains, means):
            ch["d"] = ch["y"] - mean * (1.0 / HEAD_DIM)
        variances = yield [((ch["d"] * ch["d"]).astype(BF16), head_ones) for ch in chains]
        for ch, var in zip(chains, variances):
            sl = ch["sl"]
            y = ch["d"] * lax.rsqrt(var * (1.0 / HEAD_DIM) + GN_EPS) * gng_ref[:, sl] + gnb_ref[:, sl]
            y = y + ch["bonus"] * ch["v"]
            y = y * _silu(pr_ref[b, :, SHIFT_COLS + sl.start:SHIFT_COLS + sl.stop])
            y_ref[b, :, sl] = y.astype(y_ref.dtype)

    programs = [sequence(b) for b in range(nb)]
    requests = [next(prog) for prog in programs]
    while programs:
        products = [[None] * len(reqs) for reqs in requests]
        by_right = {}
        for j, reqs in enumerate(requests):
            for i, (left, right) in enumerate(reqs):
                by_right.setdefault(id(right), (right, []))[1].append((j, i, left))
        for right, users in by_right.values():
            tall = _mm(jnp.concatenate([left for _, _, left in users], axis=0), right)
            row = 0
            for j, i, left in users:
                products[j][i] = tall[row:row + left.shape[0]]
                row += left.shape[0]
        alive, requests = [], []
        for prog, prods in zip(programs, products):
            try:
                requests.append(prog.send(prods))
                alive.append(prog)
            except StopIteration:
                pass
        programs = alive


def _mm(a, b):
    return jnp.dot(a, b, preferred_element_type=F32)


def _mm_nt(a, b):
    return lax.dot_general(a, b, (((1,), (1,)), ((), ())), preferred_element_type=F32)


def _rwkv(pr, vfirst, mu, w0, w2, a0, a2, kks, ka, rk, gng, gnb, vres, bsz, tlen):
    nc = tlen // CHUNK
    nb = RWKV_SEQS if bsz % RWKV_SEQS == 0 else 1
    has_vres = vres is not None
    full = lambda a: pl.BlockSpec(a.shape, lambda b, t: (0,) * a.ndim)
    row_spec = lambda width: pl.BlockSpec((nb, CHUNK, width), lambda b, t: (b, t, 0))
    params = [mu, w0, w2, a0, a2, kks, ka, rk, gng, gnb]
    in_specs = [row_spec(R_IN)]
    args = [pr]
    if has_vres:
        in_specs.append(row_spec(RWKV_WIDTH))
        args.append(vfirst)
        params = params + list(vres)
    in_specs += [full(a) for a in params]
    args += params
    y_shape = jax.ShapeDtypeStruct((bsz, tlen, RWKV_WIDTH), BF16)
    if has_vres:
        out_specs, out_shape = row_spec(RWKV_WIDTH), y_shape
    else:
        vf_shape = jax.ShapeDtypeStruct((bsz, tlen, RWKV_WIDTH), F32)
        out_specs, out_shape = [row_spec(RWKV_WIDTH), row_spec(RWKV_WIDTH)], [y_shape, vf_shape]
    return pl.pallas_call(
        functools.partial(_rwkv_kernel, has_vres=has_vres),
        grid=(bsz // nb, nc),
        in_specs=in_specs,
        out_specs=out_specs,
        out_shape=out_shape,
        scratch_shapes=[pltpu.VMEM((nb, SUBLANES, SHIFT_COLS), F32),
                        pltpu.VMEM((nb * PAIRS, LANES, LANES), F32)],
        compiler_params=pltpu.CompilerParams(dimension_semantics=("parallel", "arbitrary"),
                                             vmem_limit_bytes=VMEM_LIMIT),
        name="rwkv7_vres" if has_vres else "rwkv7",
    )(*args)


OUT_SUB = 256


def _outproj_kernel(x_ref, yc_ref, ys_ref, yr_ref, w_ref, g_ref, o_ref):
    groups = [slice(r0, r0 + OUT_SUB) for r0 in range(0, x_ref.shape[0], OUT_SUB)]
    mixes = [jnp.dot(jnp.concatenate([yc_ref[rs, :], ys_ref[rs, :], yr_ref[rs, :]], axis=1), w_ref[...],
                     preferred_element_type=F32) for rs in groups]
    for rs, m in zip(groups, mixes):
        ms = jnp.mean(m * m, axis=-1, keepdims=True)
        o_ref[rs, :] = x_ref[rs, :] + m * lax.rsqrt(ms + RMS_EPS) * g_ref[...]


def _outproj(x2, yc, ys, yr, w, g, tm):
    n = x2.shape[0]
    full = lambda a: pl.BlockSpec(a.shape, lambda i: (0, 0))
    rows = lambda a: pl.BlockSpec((tm, a.shape[1]), lambda i: (i, 0))
    return pl.pallas_call(
        _outproj_kernel,
        grid=(n // tm,),
        in_specs=[rows(x2), rows(yc), rows(ys), rows(yr), full(w), full(g)],
        out_specs=rows(x2),
        out_shape=jax.ShapeDtypeStruct(x2.shape, F32),
        compiler_params=pltpu.CompilerParams(dimension_semantics=("parallel",),
                                             vmem_limit_bytes=VMEM_LIMIT),
        name="outproj",
    )(x2, yc, ys, yr, w, g)


def _pad_rows(a, rows):
    return jnp.concatenate([a, jnp.zeros((rows - a.shape[0],) + a.shape[1:], a.dtype)], axis=0)


def kernel(x, pre_norm_g, post_norm_g, w_in, w_out, conv_dw, conv_dw_b, conv_ln_g, conv_ln_b, conv_pw, conv_pw_b, rwkv_mu, rwkv_w0, rwkv_w2, rwkv_a0, rwkv_a2, rwkv_kk_scale, rwkv_ka, rwkv_rk, rwkv_gn_g, rwkv_gn_b, rwkv_v0, rwkv_v1, rwkv_v2):
    bsz, tlen, dm = x.shape
    depth = w_in.shape[0]
    assert dm == D_MODEL and tlen % BLOCK_Q == 0 and tlen % CHUNK == 0
    n = bsz * tlen
    tm = 256 if tlen % 256 == 0 else BLOCK_Q
    tm_out = 2 * OUT_SUB if n % (2 * OUT_SUB) == 0 else OUT_SUB
    assert n % tm_out == 0
    row = lambda a: a.reshape(1, -1)
    c_end = 3 * CONV_WIDTH
    s_end = c_end + 4 * SB_WIDTH
    r_end = s_end + 3 * RWKV_WIDTH
    x2 = x.reshape(n, dm)
    vfirst = None
    for l in range(depth):
        wi = w_in[l]
        wa_cols = jnp.concatenate([wi[:, r_end:r_end + 2 * LORA],
                                   jnp.zeros((dm, WA_WIDTH - 2 * LORA), wi.dtype)], axis=1)
        w_r = jnp.concatenate([wi[:, s_end:r_end], wa_cols, wi[:, r_end + 2 * LORA:]], axis=1).astype(BF16)
        conv_params = (conv_dw[l], row(conv_dw_b[l]), row(conv_ln_g[l]), row(conv_ln_b[l]),
                       conv_pw[l].astype(BF16), row(conv_pw_b[l]))
        yc, qkv, g_sb, pr = _inproj(x2, row(pre_norm_g[l]), wi[:, :c_end].astype(BF16),
                                    wi[:, c_end:s_end].astype(BF16), w_r, conv_params, tm, tlen)
        ys = _sb(qkv, g_sb, bsz, tlen)

        mu = jnp.concatenate([rwkv_mu[l], jnp.zeros((WA_WIDTH - 2 * LORA,), F32)]).reshape(1, -1)
        w2 = _pad_rows(rwkv_w2[l], WA_WIDTH).astype(BF16)
        a2 = _pad_rows(jnp.concatenate([jnp.zeros((LORA, RWKV_WIDTH), F32), rwkv_a2[l]], axis=0),
                       WA_WIDTH).astype(BF16)
        vres = None
        if l > 0:
            v1 = jnp.concatenate([rwkv_v1[l - 1], jnp.zeros((RWKV_WIDTH, LANES - LORA), F32)], axis=1)
            vres = (row(rwkv_v0[l - 1]), v1.astype(BF16), _pad_rows(rwkv_v2[l - 1], LANES).astype(BF16))
        res = _rwkv(pr.reshape(bsz, tlen, R_IN), vfirst, mu, row(rwkv_w0[l]), w2, row(rwkv_a0[l]), a2,
                    row(rwkv_kk_scale[l]), row(rwkv_ka[l]), row(rwkv_rk[l]), row(rwkv_gn_g[l]),
                    row(rwkv_gn_b[l]), vres, bsz, tlen)
        if l == 0:
            yr, vfirst = res
        else:
            yr = res
        yr = yr.reshape(n, RWKV_WIDTH)

        x2 = _outproj(x2, yc, ys, yr, w_out[l].astype(BF16), row(post_norm_g[l]), tm_out)
    return x2.reshape(bsz, tlen, dm)
```

```python
import functools
import math

import jax
import jax.numpy as jnp
from jax import lax
from jax.experimental import pallas as pl
from jax.experimental.pallas import tpu as pltpu

D_MODEL = 1024
HEAD_DIM = 64
CONV_WIDTH = 256
SB_WIDTH = 384
RWKV_WIDTH = 384
CONV_TAPS = 31
LORA = 32
BLOCK_Q = 128
RMS_EPS = 1e-6
LN_EPS = 1e-5
GN_EPS = 64e-5

LANES = 128
SUBLANES = 8
PAIRS = RWKV_WIDTH // LANES
CHUNK = 64
assert CHUNK == HEAD_DIM
RWKV_SEQS = 8
C_END = 3 * CONV_WIDTH
S_END = C_END + 4 * SB_WIDTH
R_IN = 4 * RWKV_WIDTH + 2 * LORA
D_IN = S_END + R_IN
GATE_OFF = 3 * RWKV_WIDTH + 2 * LORA
WA_WIDTH = LANES
SHIFT_COLS = 3 * RWKV_WIDTH + WA_WIDTH
VMEM_LIMIT = 56 * 1024 * 1024

F32 = jnp.float32
BF16 = jnp.bfloat16


def _dot1(a, b):
    return jnp.dot(a.astype(BF16), b.astype(BF16), preferred_element_type=F32)


def _split(a):
    hi = a.astype(BF16)
    lo = (a - hi.astype(F32)).astype(BF16)
    return hi, lo


def _dot_nt1(a, b):
    return lax.dot_general(a.astype(BF16), b.astype(BF16), (((1,), (1,)), ((), ())),
                           preferred_element_type=F32)


def _sigmoid(x):
    return 0.5 * jnp.tanh(0.5 * x) + 0.5


def _silu(x):
    return x * _sigmoid(x)


CONV_HALO = 32
CONV_SUB = 64
MXU_COLS = 256


def _proj_slabs(h, w_ref, o_ref, cols=None):
    lo, hi = cols if cols is not None else (0, w_ref.shape[1])

    def slab(c0, c1):
        o_ref[:, c0 - lo:c1 - lo] = jnp.dot(h, w_ref[:, c0:c1], preferred_element_type=F32).astype(o_ref.dtype)

    return [functools.partial(slab, c0, min(c0 + MXU_COLS, hi)) for c0 in range(lo, hi, MXU_COLS)]


def _inproj_kernel(x_ref, g_ref, w_ref, dw_ref, dwb_ref, lng_ref, lnb_ref, pw_ref, pwb_ref,
                   yc_ref, qkv_ref, gs_ref, pr_ref, pc_ref, u_ref, *, tiles_per_seq):
    tt = x_ref.shape[0]

    @pl.when(pl.program_id(0) % tiles_per_seq == 0)
    def _():
        u_ref[0:CONV_HALO, :] = jnp.zeros((CONV_HALO, CONV_WIDTH), F32)

    x = x_ref[...]
    ms = jnp.mean(x * x, axis=-1, keepdims=True)
    h = (x * lax.rsqrt(ms + RMS_EPS) * g_ref[...]).astype(BF16)
    for job in _proj_slabs(h, w_ref, pc_ref, (0, C_END)):
        job()
    u_ref[CONV_HALO:CONV_HALO + tt, :] = pc_ref[:, 0:CONV_WIDTH] * _sigmoid(pc_ref[:, CONV_WIDTH:2 * CONV_WIDTH])
    slabs = (_proj_slabs(h, w_ref, qkv_ref, (C_END, C_END + 3 * SB_WIDTH))
             + _proj_slabs(h, w_ref, gs_ref, (C_END + 3 * SB_WIDTH, S_END)) + _proj_slabs(h, w_ref, pr_ref, (S_END, D_IN)))
    groups = tt // CONV_SUB
    for r0 in range(0, tt, CONV_SUB):
        gi = r0 // CONV_SUB
        for job in slabs[gi * len(slabs) // groups:(gi + 1) * len(slabs) // groups]:
            job()
        halves = []
        for l0 in range(0, CONV_WIDTH, LANES):
            ls = slice(l0, l0 + LANES)
            window = u_ref[r0:r0 + CONV_HALO + CONV_SUB, ls]
            acc = jnp.broadcast_to(dwb_ref[:, ls], (CONV_SUB, LANES))
            for b in range(SUBLANES):
                part = None
                for a in range((CONV_TAPS - 1 - b) // SUBLANES + 1):
                    tap = CONV_TAPS - 1 - (SUBLANES * a + b)
                    s = CONV_HALO - SUBLANES * (a + 1)
                    term = dw_ref[tap:tap + 1, ls] * window[s:s + CONV_SUB + SUBLANES, :]
                    part = term if part is None else part + term
                acc = acc + part[SUBLANES - b:SUBLANES - b + CONV_SUB, :]
            halves.append(acc)
        acc = jnp.concatenate(halves, axis=1)
        mean = jnp.mean(acc, axis=-1, keepdims=True)
        d = acc - mean
        var = jnp.mean(d * d, axis=-1, keepdims=True)
        y = _silu(d * lax.rsqrt(var + LN_EPS) * lng_ref[...] + lnb_ref[...])
        y = _dot1(y, pw_ref[...]) + pwb_ref[...]
        y = y * _silu(pc_ref[r0:r0 + CONV_SUB, 2 * CONV_WIDTH:3 * CONV_WIDTH])
        yc_ref[r0:r0 + CONV_SUB, :] = y.astype(yc_ref.dtype)
    u_ref[0:CONV_HALO, :] = u_ref[tt:tt + CONV_HALO, :]


def _inproj(x2, g, w, conv_params, tm, tlen):
    n = x2.shape[0]
    full = lambda a: pl.BlockSpec(a.shape, lambda i: (0, 0))
    rows = lambda width: pl.BlockSpec((tm, width), lambda i: (i, 0))
    widths = (CONV_WIDTH, 3 * SB_WIDTH, SB_WIDTH, R_IN)
    dtypes = (BF16, BF16, F32, F32)
    return pl.pallas_call(
        functools.partial(_inproj_kernel, tiles_per_seq=tlen // tm),
        grid=(n // tm,),
        in_specs=[rows(D_MODEL), full(g), full(w)] + [full(a) for a in conv_params],
        out_specs=[rows(wd) for wd in widths],
        out_shape=[jax.ShapeDtypeStruct((n, wd), dt) for wd, dt in zip(widths, dtypes)],
        scratch_shapes=[pltpu.VMEM((tm, C_END), F32),
                        pltpu.VMEM((CONV_HALO + tm, CONV_WIDTH), F32)],
        compiler_params=pltpu.CompilerParams(dimension_semantics=("arbitrary",),
                                             vmem_limit_bytes=VMEM_LIMIT),
        name="inproj_conv",
    )(x2, g, w, *conv_params)


SB_DEAD_LOG = -150.0
SB_EAGER = 2


def _sb_kernel(q_ref, k_ref, v_ref, g_ref, o_ref, carry_ref, acc_ref):
    i = pl.program_id(1)
    bq = BLOCK_Q
    lane = lax.broadcasted_iota(jnp.int32, (bq, LANES), 1)
    head0 = lane < HEAD_DIM
    row = lax.broadcasted_iota(jnp.int32, (LANES, 2 * LANES), 0)
    col = lax.broadcasted_iota(jnp.int32, (LANES, 2 * LANES), 1)
    cs = jnp.where((col >= LANES) | (row > col), -1.0, 0.0).astype(BF16)
    pairs = range(PAIRS)
    sls = [slice(p * LANES, (p + 1) * LANES) for p in pairs]

    def stacked_q(p):
        q = q_ref[:, sls[p]].astype(F32) * (HEAD_DIM ** -0.5)
        return jnp.concatenate([jnp.where(head0, q, 0.0), jnp.where(head0, 0.0, q)], axis=0).astype(BF16)

    def process(js, first):
        starts = [pl.multiple_of(j * bq, bq) for j in js]
        items = [(n, p) for n in range(len(js)) for p in pairs]
        qs = [stacked_q(p) for p in pairs]
        zs = {(n, p): _mm_nt(qs[p], k_ref[pl.ds(starts[n], bq), sls[p]]) for n, p in items}
        if first:
            qi = lax.broadcasted_iota(jnp.int32, (2 * bq, bq), 0) % bq
            ki = lax.broadcasted_iota(jnp.int32, (2 * bq, bq), 1)
            causal = ki < qi
        log_betas, sums = {}, {}
        for n, p in items:
            z = zs[n, p]
            neg_log_keep = jnp.maximum(z, 0.0) + jnp.log(1.0 + jnp.exp(-jnp.abs(z)))
            log_betas[n, p] = z - neg_log_keep
            if first and n == 0:
                neg_log_keep = jnp.where(causal, neg_log_keep, 0.0)
            sums[n, p] = _dot1(neg_log_keep, cs)
        live = None
        pvs = {}
        for p in pairs:
            carry = None if first else carry_ref[p]
            for n in range(len(js)):
                rest = sums[n, p][:, :LANES]
                if carry is not None:
                    rest = rest + carry
                att = jnp.exp(log_betas[n, p] + rest)
                if first and n == 0:
                    att = jnp.where(causal, att, 0.0)
                row_sum = sums[n, p][:, LANES:]
                carry = row_sum if carry is None else carry + row_sum
                pvs[n, p] = _dot1(att, v_ref[pl.ds(starts[n], bq), sls[p]])
            carry_ref[p] = carry
            m = jnp.max(carry)
            live = m if live is None else jnp.maximum(live, m)
        for p in pairs:
            tot = pvs[0, p]
            for n in range(1, len(js)):
                tot = tot + pvs[n, p]
            if first:
                acc_ref[p] = tot
            else:
                acc_ref[p] += tot
        return live

    def eager(_):
        return jnp.int32(SB_EAGER), process([i - n for n in range(SB_EAGER + 1)], True)

    def diagonal_only(_):
        return jnp.int32(0), process([i], True)

    done, live = lax.cond(i >= SB_EAGER, eager, diagonal_only, None)

    def cond(c):
        return (c[0] < i) & (c[1] > SB_DEAD_LOG)

    def body(c):
        return c[0] + 1, process([i - 1 - c[0]], False)

    lax.while_loop(cond, body, (done, live))
    for p in range(PAIRS):
        sl = slice(p * LANES, (p + 1) * LANES)
        o = jnp.where(head0, acc_ref[p, :bq], acc_ref[p, bq:])
        o_ref[:, sl] = (o * _silu(g_ref[:, sl])).astype(o_ref.dtype)


def _sb(qkv, gate, bsz, tlen):
    nq = tlen // BLOCK_Q
    return pl.pallas_call(
        _sb_kernel,
        grid=(bsz, nq),
        in_specs=[pl.BlockSpec((BLOCK_Q, SB_WIDTH), lambda b, i: (b * nq + i, 0)),
                  pl.BlockSpec((tlen, SB_WIDTH), lambda b, i: (b, 1)),
                  pl.BlockSpec((tlen, SB_WIDTH), lambda b, i: (b, 2)),
                  pl.BlockSpec((BLOCK_Q, SB_WIDTH), lambda b, i: (b * nq + i, 0))],
        out_specs=pl.BlockSpec((BLOCK_Q, SB_WIDTH), lambda b, i: (b * nq + i, 0)),
        out_shape=jax.ShapeDtypeStruct((bsz * tlen, SB_WIDTH), BF16),
        scratch_shapes=[pltpu.VMEM((PAIRS, 2 * BLOCK_Q, LANES), F32),
                        pltpu.VMEM((PAIRS, 2 * BLOCK_Q, LANES), F32)],
        compiler_params=pltpu.CompilerParams(dimension_semantics=("parallel", "arbitrary"),
                                             vmem_limit_bytes=VMEM_LIMIT),
        name="stickbreak",
    )(qkv, qkv, qkv, gate)


def _stack_heads(a, head0):
    return jnp.concatenate([jnp.where(head0, a, 0.0), jnp.where(head0, 0.0, a)], axis=0)


def _rwkv_kernel(*refs, has_vres):
    if has_vres:
        (pr_ref, vf_ref, mu_ref, w0_ref, w2_ref, a0_ref, a2_ref, kks_ref, ka_ref, rk_ref, gng_ref, gnb_ref,
         v0_ref, v1_ref, v2_ref, y_ref, xs_ref, st_ref) = refs
    else:
        (pr_ref, mu_ref, w0_ref, w2_ref, a0_ref, a2_ref, kks_ref, ka_ref, rk_ref, gng_ref, gnb_ref,
         y_ref, vf_out_ref, xs_ref, st_ref) = refs
    c = CHUNK
    w = RWKV_WIDTH
    nb = pr_ref.shape[0]

    @pl.when(pl.program_id(1) == 0)
    def _():
        xs_ref[...] = jnp.zeros(xs_ref.shape, F32)
        st_ref[...] = jnp.zeros(st_ref.shape, F32)

    lane = lax.broadcasted_iota(jnp.int32, (c, LANES), 1)
    head0 = lane < HEAD_DIM
    ri = lax.broadcasted_iota(jnp.int32, (LANES, LANES), 0)
    ci = lax.broadcasted_iota(jnp.int32, (LANES, LANES), 1)
    same_head = (ri < HEAD_DIM) == (ci < HEAD_DIM)
    head_ones = jnp.where(same_head, 1.0, 0.0).astype(BF16)
    weights = dict(w2=w2_ref[...], a2=a2_ref[...])
    if has_vres:
        weights.update(v1=v1_ref[...], v2=v2_ref[...])
    ti = lax.broadcasted_iota(jnp.int32, (c, LANES), 0)
    ji = lane % HEAD_DIM
    lower = ti > ji
    lower_eq = ti >= ji
    eye = jnp.where(ti == ji, 1.0, 0.0)
    rc = lax.broadcasted_iota(jnp.int32, (c, c), 0)
    cc = lax.broadcasted_iota(jnp.int32, (c, c), 1)
    cum = jnp.where(rc >= cc, 1.0, 0.0).astype(BF16)

    def sequence(b):
        x = pr_ref[b, :, 0:SHIFT_COLS]
        first_row = lax.broadcasted_iota(jnp.int32, (c, SHIFT_COLS), 0) == 0
        prev = jnp.where(first_row, xs_ref[b, 0:1, :], pltpu.roll(x, 1, axis=0))
        xs_ref[b, 0:1, :] = x[c - 1:c, :]
        f = x + (prev - x) * mu_ref[...]
        r = f[:, 0:w]
        k = f[:, w:2 * w]
        v = f[:, 2 * w:3 * w]
        wa = f[:, 3 * w:3 * w + WA_WIDTH]
        lora = [(jnp.tanh(wa).astype(BF16), weights["w2"]), (wa.astype(BF16), weights["a2"])]
        if has_vres:
            lora.append((v.astype(BF16), weights["v1"]))
        products = yield lora
        logw = (-math.exp(-0.5)) * _sigmoid(w0_ref[...] + products[0])
        a_lr = _sigmoid(a0_ref[...] + products[1])
        lw_hi, lw_lo = _split(logw)
        logp = _mm(cum, lw_hi) + _mm(cum, lw_lo)
        if has_vres:
            (v_up,) = yield [(products[2].astype(BF16), weights["v2"])]
            v = v + (vf_ref[b] - v) * _sigmoid(v0_ref[...] + v_up)
        else:
            vf_out_ref[b] = v
            yield []
        p = jnp.exp(logp)
        p_prev = jnp.exp(logp - logw)
        inv_p = jnp.exp(-logp)
        kk = k * kks_ref[...]
        k2 = k * (1.0 + (a_lr - 1.0) * ka_ref[...])
        sls = [slice(pr * LANES, (pr + 1) * LANES) for pr in range(PAIRS)]
        n2s = yield [((kk[:, sl] * kk[:, sl]).astype(BF16), head_ones) for sl in sls]
        stack = lambda a: _stack_heads(a, head0).astype(BF16)
        chains = []
        for pr, sl in enumerate(sls):
            kk_p = kk[:, sl] * lax.rsqrt(jnp.maximum(n2s[pr], 1e-24))
            a_n = -kk_p * p_prev[:, sl]
            b_n = kk_p * a_lr[:, sl] * inv_p[:, sl]
            k_n = k2[:, sl] * inv_p[:, sl]
            r_n = r[:, sl] * p[:, sl]
            pe = p[c - 1:c, sl]
            ar = jnp.concatenate([a_n, r_n], axis=0).astype(BF16)
            ch = dict(sl=sl, idx=b * PAIRS + pr, ar=ar, pe=pe, v_st=stack(v[:, sl]),
                      be_ke=jnp.concatenate([(b_n * pe).T, (k_n * pe).T], axis=1).astype(BF16),
                      rkk=(r[:, sl] * k2[:, sl] * rk_ref[:, sl]).astype(BF16), v=v[:, sl])
            ch["gram"] = _mm_nt(ar, jnp.concatenate([stack(b_n), stack(k_n)], axis=0))
            chains.append(ch)
        yield []
        requests = []
        for ch in chains:
            gram = ch.pop("gram")
            ab = jnp.where(lower, gram[:c, :LANES], 0.0)
            ak = jnp.where(lower, gram[:c, LANES:], 0.0).astype(BF16)
            ch["rb_rk"] = jnp.concatenate([jnp.where(lower_eq, gram[c:, :LANES], 0.0),
                                           jnp.where(lower_eq, gram[c:, LANES:], 0.0)], axis=1).astype(BF16)
            g = st_ref[ch["idx"]]
            ch["g"] = g
            ch["s"] = eye + ab
            requests += [(ak, ch["v_st"]), (ch["ar"], g.astype(BF16)), (ab.astype(BF16), stack(ab))]
        products = yield requests
        for i, ch in enumerate(chains):
            ch["akv"], ch["ar_g"], ch["pw"] = products[3 * i:3 * i + 3]
        for _ in range(4):
            for ch in chains:
                res = _mm(ch["pw"].astype(BF16), jnp.concatenate([stack(ch["s"]), stack(ch["pw"])], axis=1))
                ch["s"] = ch["s"] + res[:, :LANES]
                ch["pw"] = res[:, LANES:]
            yield []
        products = yield [(ch["pw"].astype(BF16), stack(ch["s"])) for ch in chains]
        for ch, prod in zip(chains, products):
            ch["s"] = ch["s"] + prod
        products = yield [(ch["s"].astype(BF16), stack(ch["ar_g"][:c] + ch["akv"])) for ch in chains]
        requests = []
        for ch, u in zip(chains, products):
            ch["y_in"] = _mm(ch["rb_rk"], jnp.concatenate([stack(u), ch["v_st"]], axis=0))
            requests += [(ch["be_ke"], jnp.concatenate([u.astype(BF16), ch["v"].astype(BF16)], axis=0)),
                         (ch["rkk"], head_ones)]
        products = yield requests
        requests = []
        for i, ch in enumerate(chains):
            inc, ch["bonus"] = products[2 * i:2 * i + 2]
            pcol = jnp.broadcast_to(ch["pe"], (LANES, LANES)).T
            st_ref[ch["idx"]] = pcol * ch["g"] + jnp.where(same_head, inc, 0.0)
            ch["y"] = ch["ar_g"][c:] + ch["y_in"]
            requests.append((ch["y"].astype(BF16), head_ones))
        means = yield requests
        for ch, mean in zip(chains, means):
            ch["d"] = ch["y"] - mean * (1.0 / HEAD_DIM)
        variances = yield [((ch["d"] * ch["d"]).astype(BF16), head_ones) for ch in chains]
        for ch, var in zip(chains, variances):
            sl = ch["sl"]
            y = ch["d"] * lax.rsqrt(var * (1.0 / HEAD_DIM) + GN_EPS) * gng_ref[:, sl] + gnb_ref[:, sl]
            y = y + ch["bonus"] * ch["v"]
            y = y * _silu(pr_ref[b, :, GATE_OFF + sl.start:GATE_OFF + sl.stop])
            y_ref[b, :, sl] = y.astype(y_ref.dtype)

    programs = [sequence(b) for b in range(nb)]
    requests = [next(prog) for prog in programs]
    while programs:
        products = [[None] * len(reqs) for reqs in requests]
        by_right = {}
        for j, reqs in enumerate(requests):
            for i, (left, right) in enumerate(reqs):
                by_right.setdefault(id(right), (right, []))[1].append((j, i, left))
        for right, users in by_right.values():
            tall = _mm(jnp.concatenate([left for _, _, left in users], axis=0), right)
            row = 0
            for j, i, left in users:
                products[j][i] = tall[row:row + left.shape[0]]
                row += left.shape[0]
        alive, requests = [], []
        for prog, prods in zip(programs, products):
            try:
                requests.append(prog.send(prods))
                alive.append(prog)
            except StopIteration:
                pass
        programs = alive


def _mm(a, b):
    return jnp.dot(a, b, preferred_element_type=F32)


def _mm_nt(a, b):
    return lax.dot_general(a, b, (((1,), (1,)), ((), ())), preferred_element_type=F32)


def _rwkv(pr, vfirst, mu, w0, w2, a0, a2, kks, ka, rk, gng, gnb, vres, bsz, tlen):
    nc = tlen // CHUNK
    nb = RWKV_SEQS if bsz % RWKV_SEQS == 0 else 1
    has_vres = vres is not None
    full = lambda a: pl.BlockSpec(a.shape, lambda b, t: (0,) * a.ndim)
    row_spec = lambda width: pl.BlockSpec((nb, CHUNK, width), lambda b, t: (b, t, 0))
    params = [mu, w0, w2, a0, a2, kks, ka, rk, gng, gnb]
    in_specs = [row_spec(R_IN)]
    args = [pr]
    if has_vres:
        in_specs.append(row_spec(RWKV_WIDTH))
        args.append(vfirst)
        params = params + list(vres)
    in_specs += [full(a) for a in params]
    args += params
    y_shape = jax.ShapeDtypeStruct((bsz, tlen, RWKV_WIDTH), BF16)
    if has_vres:
        out_specs, out_shape = row_spec(RWKV_WIDTH), y_shape
    else:
        vf_shape = jax.ShapeDtypeStruct((bsz, tlen, RWKV_WIDTH), F32)
        out_specs, out_shape = [row_spec(RWKV_WIDTH), row_spec(RWKV_WIDTH)], [y_shape, vf_shape]
    return pl.pallas_call(
        functools.partial(_rwkv_kernel, has_vres=has_vres),
        grid=(bsz // nb, nc),
        in_specs=in_specs,
        out_specs=out_specs,
        out_shape=out_shape,
        scratch_shapes=[pltpu.VMEM((nb, SUBLANES, SHIFT_COLS), F32),
                        pltpu.VMEM((nb * PAIRS, LANES, LANES), F32)],
        compiler_params=pltpu.CompilerParams(dimension_semantics=("parallel", "arbitrary"),
                                             vmem_limit_bytes=VMEM_LIMIT),
        name="rwkv7_vres" if has_vres else "rwkv7",
    )(*args)


OUT_SUB = 256


def _outproj_kernel(x_ref, yc_ref, ys_ref, yr_ref, w_ref, g_ref, o_ref):
    groups = [slice(r0, r0 + OUT_SUB) for r0 in range(0, x_ref.shape[0], OUT_SUB)]
    mixes = [jnp.dot(jnp.concatenate([yc_ref[rs, :], ys_ref[rs, :], yr_ref[rs, :]], axis=1), w_ref[...],
                     preferred_element_type=F32) for rs in groups]
    for rs, m in zip(groups, mixes):
        ms = jnp.mean(m * m, axis=-1, keepdims=True)
        o_ref[rs, :] = x_ref[rs, :] + m * lax.rsqrt(ms + RMS_EPS) * g_ref[...]


def _outproj(x2, yc, ys, yr, w, g, tm):
    n = x2.shape[0]
    full = lambda a: pl.BlockSpec(a.shape, lambda i: (0, 0))
    rows = lambda a: pl.BlockSpec((tm, a.shape[1]), lambda i: (i, 0))
    return pl.pallas_call(
        _outproj_kernel,
        grid=(n // tm,),
        in_specs=[rows(x2), rows(yc), rows(ys), rows(yr), full(w), full(g)],
        out_specs=rows(x2),
        out_shape=jax.ShapeDtypeStruct(x2.shape, F32),
        compiler_params=pltpu.CompilerParams(dimension_semantics=("parallel",),
                                             vmem_limit_bytes=VMEM_LIMIT),
        name="outproj",
    )(x2, yc, ys, yr, w, g)


def _pad_rows(a, rows):
    return jnp.concatenate([a, jnp.zeros((rows - a.shape[0],) + a.shape[1:], a.dtype)], axis=0)


def kernel(x, pre_norm_g, post_norm_g, w_in, w_out, conv_dw, conv_dw_b, conv_ln_g, conv_ln_b, conv_pw, conv_pw_b, rwkv_mu, rwkv_w0, rwkv_w2, rwkv_a0, rwkv_a2, rwkv_kk_scale, rwkv_ka, rwkv_rk, rwkv_gn_g, rwkv_gn_b, rwkv_v0, rwkv_v1, rwkv_v2):
    bsz, tlen, dm = x.shape
    depth = w_in.shape[0]
    assert dm == D_MODEL and tlen % BLOCK_Q == 0 and tlen % CHUNK == 0
    n = bsz * tlen
    tm = 256 if tlen % 256 == 0 else BLOCK_Q
    tm_out = next(g * OUT_SUB for g in (4, 2, 1) if n % (g * OUT_SUB) == 0)
    row = lambda a: a.reshape(1, -1)
    assert w_in.shape[2] == D_IN
    x2 = x.reshape(n, dm)
    vfirst = None
    for l in range(depth):
        conv_params = (conv_dw[l], row(conv_dw_b[l]), row(conv_ln_g[l]), row(conv_ln_b[l]),
                       conv_pw[l].astype(BF16), row(conv_pw_b[l]))
        yc, qkv, g_sb, pr = _inproj(x2, row(pre_norm_g[l]), w_in[l].astype(BF16), conv_params, tm, tlen)
        ys = _sb(qkv, g_sb, bsz, tlen)

        mu = jnp.concatenate([rwkv_mu[l], jnp.zeros((WA_WIDTH - 2 * LORA,), F32)]).reshape(1, -1)
        w2 = _pad_rows(rwkv_w2[l], WA_WIDTH).astype(BF16)
        a2 = _pad_rows(jnp.concatenate([jnp.zeros((LORA, RWKV_WIDTH), F32), rwkv_a2[l]], axis=0),
                       WA_WIDTH).astype(BF16)
        vres = None
        if l > 0:
            v1 = jnp.concatenate([rwkv_v1[l - 1], jnp.zeros((RWKV_WIDTH, LANES - LORA), F32)], axis=1)
            vres = (row(rwkv_v0[l - 1]), v1.astype(BF16), _pad_rows(rwkv_v2[l - 1], LANES).astype(BF16))
        res = _rwkv(pr.reshape(bsz, tlen, R_IN), vfirst, mu, row(rwkv_w0[l]), w2, row(rwkv_a0[l]), a2,
                    row(rwkv_kk_scale[l]), row(rwkv_ka[l]), row(rwkv_rk[l]), row(rwkv_gn_g[l]),
                    row(rwkv_gn_b[l]), vres, bsz, tlen)
        if l == 0:
            yr, vfirst = res
        else:
            yr = res
        yr = yr.reshape(n, RWKV_WIDTH)

        x2 = _outproj(x2, yc, ys, yr, w_out[l].astype(BF16), row(post_norm_g[l]), tm_out)
    return x2.reshape(bsz, tlen, dm)
```

```python
import functools
import math

import jax
import jax.numpy as jnp
from jax import lax
from jax.experimental import pallas as pl
from jax.experimental.pallas import tpu as pltpu

D_MODEL = 1024
HEAD_DIM = 64
CONV_WIDTH = 256
SB_WIDTH = 384
RWKV_WIDTH = 384
CONV_TAPS = 31
LORA = 32
BLOCK_Q = 128
RMS_EPS = 1e-6
LN_EPS = 1e-5
GN_EPS = 64e-5

LANES = 128
SUBLANES = 8
PAIRS = RWKV_WIDTH // LANES
CHUNK = 64
assert CHUNK == HEAD_DIM
RWKV_SEQS = 8
C_END = 3 * CONV_WIDTH
S_END = C_END + 4 * SB_WIDTH
R_IN = 4 * RWKV_WIDTH + 2 * LORA
D_IN = S_END + R_IN
GATE_OFF = 3 * RWKV_WIDTH + 2 * LORA
WA_WIDTH = LANES
SHIFT_COLS = 3 * RWKV_WIDTH + WA_WIDTH
VMEM_LIMIT = 56 * 1024 * 1024

F32 = jnp.float32
BF16 = jnp.bfloat16


def _dot1(a, b):
    return jnp.dot(a.astype(BF16), b.astype(BF16), preferred_element_type=F32)


def _split(a):
    hi = a.astype(BF16)
    lo = (a - hi.astype(F32)).astype(BF16)
    return hi, lo


def _dot_nt1(a, b):
    return lax.dot_general(a.astype(BF16), b.astype(BF16), (((1,), (1,)), ((), ())),
                           preferred_element_type=F32)


def _sigmoid(x):
    return 0.5 * jnp.tanh(0.5 * x) + 0.5


def _silu(x):
    return x * _sigmoid(x)


CONV_HALO = 32
CONV_SUB = 64
MXU_COLS = 256


def _proj_slabs(h, w_ref, o_ref, cols=None):
    lo, hi = cols if cols is not None else (0, w_ref.shape[1])

    def slab(c0, c1):
        o_ref[:, c0 - lo:c1 - lo] = jnp.dot(h, w_ref[:, c0:c1], preferred_element_type=F32).astype(o_ref.dtype)

    return [functools.partial(slab, c0, min(c0 + MXU_COLS, hi)) for c0 in range(lo, hi, MXU_COLS)]


def _inproj_kernel(x_ref, g_ref, w_ref, dw_ref, dwb_ref, lng_ref, lnb_ref, pw_ref, pwb_ref,
                   yc_ref, qkv_ref, gs_ref, pr_ref, pc_ref, u_ref, *, tiles_per_seq):
    tt = x_ref.shape[0]

    @pl.when(pl.program_id(0) % tiles_per_seq == 0)
    def _():
        u_ref[0:CONV_HALO, :] = jnp.zeros((CONV_HALO, CONV_WIDTH), F32)

    x = x_ref[...]
    ms = jnp.mean(x * x, axis=-1, keepdims=True)
    h = (x * lax.rsqrt(ms + RMS_EPS) * g_ref[...]).astype(BF16)
    for job in _proj_slabs(h, w_ref, pc_ref, (0, C_END)):
        job()
    u_ref[CONV_HALO:CONV_HALO + tt, :] = pc_ref[:, 0:CONV_WIDTH] * _sigmoid(pc_ref[:, CONV_WIDTH:2 * CONV_WIDTH])
    slabs = (_proj_slabs(h, w_ref, qkv_ref, (C_END, C_END + 3 * SB_WIDTH))
             + _proj_slabs(h, w_ref, gs_ref, (C_END + 3 * SB_WIDTH, S_END)) + _proj_slabs(h, w_ref, pr_ref, (S_END, D_IN)))
    groups = tt // CONV_SUB
    for r0 in range(0, tt, CONV_SUB):
        gi = r0 // CONV_SUB
        for job in slabs[gi * len(slabs) // groups:(gi + 1) * len(slabs) // groups]:
            job()
        halves = []
        for l0 in range(0, CONV_WIDTH, LANES):
            ls = slice(l0, l0 + LANES)
            window = u_ref[r0:r0 + CONV_HALO + CONV_SUB, ls]
            acc = jnp.broadcast_to(dwb_ref[:, ls], (CONV_SUB, LANES))
            for b in range(SUBLANES):
                part = None
                for a in range((CONV_TAPS - 1 - b) // SUBLANES + 1):
                    tap = CONV_TAPS - 1 - (SUBLANES * a + b)
                    s = CONV_HALO - SUBLANES * (a + 1)
                    term = dw_ref[tap:tap + 1, ls] * window[s:s + CONV_SUB + SUBLANES, :]
                    part = term if part is None else part + term
                acc = acc + part[SUBLANES - b:SUBLANES - b + CONV_SUB, :]
            halves.append(acc)
        acc = jnp.concatenate(halves, axis=1)
        mean = jnp.mean(acc, axis=-1, keepdims=True)
        d = acc - mean
        var = jnp.mean(d * d, axis=-1, keepdims=True)
        y = _silu(d * lax.rsqrt(var + LN_EPS) * lng_ref[...] + lnb_ref[...])
        y = _dot1(y, pw_ref[...]) + pwb_ref[...]
        y = y * _silu(pc_ref[r0:r0 + CONV_SUB, 2 * CONV_WIDTH:3 * CONV_WIDTH])
        yc_ref[r0:r0 + CONV_SUB, :] = y.astype(yc_ref.dtype)
    u_ref[0:CONV_HALO, :] = u_ref[tt:tt + CONV_HALO, :]


def _layer_weight(w, layer):
    return pl.BlockSpec((None,) + w.shape[1:], lambda i: (layer, 0, 0))


def _inproj(x2, g, w, layer, conv_params, tm, tlen):
    n = x2.shape[0]
    full = lambda a: pl.BlockSpec(a.shape, lambda i: (0, 0))
    rows = lambda width: pl.BlockSpec((tm, width), lambda i: (i, 0))
    widths = (CONV_WIDTH, 3 * SB_WIDTH, SB_WIDTH, R_IN)
    dtypes = (BF16, BF16, F32, F32)
    return pl.pallas_call(
        functools.partial(_inproj_kernel, tiles_per_seq=tlen // tm),
        grid=(n // tm,),
        in_specs=[rows(D_MODEL), full(g), _layer_weight(w, layer)] + [full(a) for a in conv_params],
        out_specs=[rows(wd) for wd in widths],
        out_shape=[jax.ShapeDtypeStruct((n, wd), dt) for wd, dt in zip(widths, dtypes)],
        scratch_shapes=[pltpu.VMEM((tm, C_END), F32),
                        pltpu.VMEM((CONV_HALO + tm, CONV_WIDTH), F32)],
        compiler_params=pltpu.CompilerParams(dimension_semantics=("arbitrary",),
                                             vmem_limit_bytes=VMEM_LIMIT),
        name="inproj_conv",
    )(x2, g, w, *conv_params)


SB_DEAD_LOG = -150.0
SB_EAGER = 2


def _sb_kernel(q_ref, k_ref, v_ref, g_ref, o_ref, carry_ref, acc_ref):
    i = pl.program_id(1)
    bq = BLOCK_Q
    lane = lax.broadcasted_iota(jnp.int32, (bq, LANES), 1)
    head0 = lane < HEAD_DIM
    row = lax.broadcasted_iota(jnp.int32, (LANES, 2 * LANES), 0)
    col = lax.broadcasted_iota(jnp.int32, (LANES, 2 * LANES), 1)
    cs = jnp.where((col >= LANES) | (row > col), -1.0, 0.0).astype(BF16)
    pairs = range(PAIRS)
    sls = [slice(p * LANES, (p + 1) * LANES) for p in pairs]

    def stacked_q(p):
        q = q_ref[:, sls[p]].astype(F32) * (HEAD_DIM ** -0.5)
        return jnp.concatenate([jnp.where(head0, q, 0.0), jnp.where(head0, 0.0, q)], axis=0).astype(BF16)

    def process(js, first):
        starts = [pl.multiple_of(j * bq, bq) for j in js]
        items = [(n, p) for n in range(len(js)) for p in pairs]
        qs = [stacked_q(p) for p in pairs]
        zs = {(n, p): _mm_nt(qs[p], k_ref[pl.ds(starts[n], bq), sls[p]]) for n, p in items}
        if first:
            qi = lax.broadcasted_iota(jnp.int32, (2 * bq, bq), 0) % bq
            ki = lax.broadcasted_iota(jnp.int32, (2 * bq, bq), 1)
            causal = ki < qi
        log_betas, sums = {}, {}
        for n, p in items:
            z = zs[n, p]
            neg_log_keep = jnp.maximum(z, 0.0) + jnp.log(1.0 + jnp.exp(-jnp.abs(z)))
            log_betas[n, p] = z - neg_log_keep
            if first and n == 0:
                neg_log_keep = jnp.where(causal, neg_log_keep, 0.0)
            sums[n, p] = _dot1(neg_log_keep, cs)
        live = None
        pvs = {}
        for p in pairs:
            carry = None if first else carry_ref[p]
            for n in range(len(js)):
                rest = sums[n, p][:, :LANES]
                if carry is not None:
                    rest = rest + carry
                att = jnp.exp(log_betas[n, p] + rest)
                if first and n == 0:
                    att = jnp.where(causal, att, 0.0)
                row_sum = sums[n, p][:, LANES:]
                carry = row_sum if carry is None else carry + row_sum
                pvs[n, p] = _dot1(att, v_ref[pl.ds(starts[n], bq), sls[p]])
            carry_ref[p] = carry
            m = jnp.max(carry)
            live = m if live is None else jnp.maximum(live, m)
        for p in pairs:
            tot = pvs[0, p]
            for n in range(1, len(js)):
                tot = tot + pvs[n, p]
            if first:
                acc_ref[p] = tot
            else:
                acc_ref[p] += tot
        return live

    def eager(_):
        return jnp.int32(SB_EAGER), process([i - n for n in range(SB_EAGER + 1)], True)

    def diagonal_only(_):
        return jnp.int32(0), process([i], True)

    done, live = lax.cond(i >= SB_EAGER, eager, diagonal_only, None)

    def cond(c):
        return (c[0] < i) & (c[1] > SB_DEAD_LOG)

    def body(c):
        return c[0] + 1, process([i - 1 - c[0]], False)

    lax.while_loop(cond, body, (done, live))
    for p in range(PAIRS):
        sl = slice(p * LANES, (p + 1) * LANES)
        o = jnp.where(head0, acc_ref[p, :bq], acc_ref[p, bq:])
        o_ref[:, sl] = (o * _silu(g_ref[:, sl])).astype(o_ref.dtype)


def _sb(qkv, gate, bsz, tlen):
    nq = tlen // BLOCK_Q
    return pl.pallas_call(
        _sb_kernel,
        grid=(bsz, nq),
        in_specs=[pl.BlockSpec((BLOCK_Q, SB_WIDTH), lambda b, i: (b * nq + i, 0)),
                  pl.BlockSpec((tlen, SB_WIDTH), lambda b, i: (b, 1)),
                  pl.BlockSpec((tlen, SB_WIDTH), lambda b, i: (b, 2)),
                  pl.BlockSpec((BLOCK_Q, SB_WIDTH), lambda b, i: (b * nq + i, 0))],
        out_specs=pl.BlockSpec((BLOCK_Q, SB_WIDTH), lambda b, i: (b * nq + i, 0)),
        out_shape=jax.ShapeDtypeStruct((bsz * tlen, SB_WIDTH), BF16),
        scratch_shapes=[pltpu.VMEM((PAIRS, 2 * BLOCK_Q, LANES), F32),
                        pltpu.VMEM((PAIRS, 2 * BLOCK_Q, LANES), F32)],
        compiler_params=pltpu.CompilerParams(dimension_semantics=("parallel", "arbitrary"),
                                             vmem_limit_bytes=VMEM_LIMIT),
        name="stickbreak",
    )(qkv, qkv, qkv, gate)


def _stack_heads(a, head0):
    return jnp.concatenate([jnp.where(head0, a, 0.0), jnp.where(head0, 0.0, a)], axis=0)


def _rwkv_kernel(*refs, has_vres):
    if has_vres:
        (pr_ref, vf_ref, mu_ref, w0_ref, w2_ref, a0_ref, a2_ref, kks_ref, ka_ref, rk_ref, gng_ref, gnb_ref,
         v0_ref, v1_ref, v2_ref, y_ref, xs_ref, st_ref) = refs
    else:
        (pr_ref, mu_ref, w0_ref, w2_ref, a0_ref, a2_ref, kks_ref, ka_ref, rk_ref, gng_ref, gnb_ref,
         y_ref, vf_out_ref, xs_ref, st_ref) = refs
    c = CHUNK
    w = RWKV_WIDTH
    nb = pr_ref.shape[0]

    @pl.when(pl.program_id(1) == 0)
    def _():
        xs_ref[...] = jnp.zeros(xs_ref.shape, F32)
        st_ref[...] = jnp.zeros(st_ref.shape, F32)

    lane = lax.broadcasted_iota(jnp.int32, (c, LANES), 1)
    head0 = lane < HEAD_DIM
    ri = lax.broadcasted_iota(jnp.int32, (LANES, LANES), 0)
    ci = lax.broadcasted_iota(jnp.int32, (LANES, LANES), 1)
    same_head = (ri < HEAD_DIM) == (ci < HEAD_DIM)
    head_ones = jnp.where(same_head, 1.0, 0.0).astype(BF16)
    weights = dict(w2=w2_ref[...], a2=a2_ref[...])
    if has_vres:
        weights.update(v1=v1_ref[...], v2=v2_ref[...])
    ti = lax.broadcasted_iota(jnp.int32, (c, LANES), 0)
    ji = lane % HEAD_DIM
    lower = ti > ji
    lower_eq = ti >= ji
    eye = jnp.where(ti == ji, 1.0, 0.0)
    rc = lax.broadcasted_iota(jnp.int32, (c, c), 0)
    cc = lax.broadcasted_iota(jnp.int32, (c, c), 1)
    cum = jnp.where(rc >= cc, 1.0, 0.0).astype(BF16)

    def sequence(b):
        x = pr_ref[b, :, 0:SHIFT_COLS]
        first_row = lax.broadcasted_iota(jnp.int32, (c, SHIFT_COLS), 0) == 0
        prev = jnp.where(first_row, xs_ref[b, 0:1, :], pltpu.roll(x, 1, axis=0))
        xs_ref[b, 0:1, :] = x[c - 1:c, :]
        f = x + (prev - x) * mu_ref[...]
        r = f[:, 0:w]
        k = f[:, w:2 * w]
        v = f[:, 2 * w:3 * w]
        wa = f[:, 3 * w:3 * w + WA_WIDTH]
        lora = [(jnp.tanh(wa).astype(BF16), weights["w2"]), (wa.astype(BF16), weights["a2"])]
        if has_vres:
            lora.append((v.astype(BF16), weights["v1"]))
        products = yield lora
        logw = (-math.exp(-0.5)) * _sigmoid(w0_ref[...] + products[0])
        a_lr = _sigmoid(a0_ref[...] + products[1])
        lw_hi, lw_lo = _split(logw)
        logp = _mm(cum, lw_hi) + _mm(cum, lw_lo)
        if has_vres:
            (v_up,) = yield [(products[2].astype(BF16), weights["v2"])]
            v = v + (vf_ref[b] - v) * _sigmoid(v0_ref[...] + v_up)
        else:
            vf_out_ref[b] = v
            yield []
        p = jnp.exp(logp)
        p_prev = jnp.exp(logp - logw)
        inv_p = jnp.exp(-logp)
        kk = k * kks_ref[...]
        k2 = k * (1.0 + (a_lr - 1.0) * ka_ref[...])
        sls = [slice(pr * LANES, (pr + 1) * LANES) for pr in range(PAIRS)]
        n2s = yield [((kk[:, sl] * kk[:, sl]).astype(BF16), head_ones) for sl in sls]
        stack = lambda a: _stack_heads(a, head0).astype(BF16)
        chains = []
        for pr, sl in enumerate(sls):
            kk_p = kk[:, sl] * lax.rsqrt(jnp.maximum(n2s[pr], 1e-24))
            a_n = -kk_p * p_prev[:, sl]
            b_n = kk_p * a_lr[:, sl] * inv_p[:, sl]
            k_n = k2[:, sl] * inv_p[:, sl]
            r_n = r[:, sl] * p[:, sl]
            pe = p[c - 1:c, sl]
            ar = jnp.concatenate([a_n, r_n], axis=0).astype(BF16)
            ch = dict(sl=sl, idx=b * PAIRS + pr, ar=ar, pe=pe, v_st=stack(v[:, sl]),
                      be_ke=jnp.concatenate([(b_n * pe).T, (k_n * pe).T], axis=1).astype(BF16),
                      rkk=(r[:, sl] * k2[:, sl] * rk_ref[:, sl]).astype(BF16), v=v[:, sl])
            ch["gram"] = _mm_nt(ar, jnp.concatenate([stack(b_n), stack(k_n)], axis=0))
            chains.append(ch)
        yield []
        requests = []
        for ch in chains:
            gram = ch.pop("gram")
            ab = jnp.where(lower, gram[:c, :LANES], 0.0)
            ak = jnp.where(lower, gram[:c, LANES:], 0.0).astype(BF16)
            ch["rb_rk"] = jnp.concatenate([jnp.where(lower_eq, gram[c:, :LANES], 0.0),
                                           jnp.where(lower_eq, gram[c:, LANES:], 0.0)], axis=1).astype(BF16)
            g = st_ref[ch["idx"]]
            ch["g"] = g
            ch["s"] = eye + ab
            requests += [(ak, ch["v_st"]), (ch["ar"], g.astype(BF16)), (ab.astype(BF16), stack(ab))]
        products = yield requests
        for i, ch in enumerate(chains):
            ch["akv"], ch["ar_g"], ch["pw"] = products[3 * i:3 * i + 3]
        for _ in range(4):
            for ch in chains:
                res = _mm(ch["pw"].astype(BF16), jnp.concatenate([stack(ch["s"]), stack(ch["pw"])], axis=1))
                ch["s"] = ch["s"] + res[:, :LANES]
                ch["pw"] = res[:, LANES:]
            yield []
        products = yield [(ch["pw"].astype(BF16), stack(ch["s"])) for ch in chains]
        for ch, prod in zip(chains, products):
            ch["s"] = ch["s"] + prod
        products = yield [(ch["s"].astype(BF16), stack(ch["ar_g"][:c] + ch["akv"])) for ch in chains]
        requests = []
        for ch, u in zip(chains, products):
            ch["y_in"] = _mm(ch["rb_rk"], jnp.concatenate([stack(u), ch["v_st"]], axis=0))
            requests += [(ch["be_ke"], jnp.concatenate([u.astype(BF16), ch["v"].astype(BF16)], axis=0)),
                         (ch["rkk"], head_ones)]
        products = yield requests
        requests = []
        for i, ch in enumerate(chains):
            inc, ch["bonus"] = products[2 * i:2 * i + 2]
            pcol = jnp.broadcast_to(ch["pe"], (LANES, LANES)).T
            st_ref[ch["idx"]] = pcol * ch["g"] + jnp.where(same_head, inc, 0.0)
            ch["y"] = ch["ar_g"][c:] + ch["y_in"]
            requests.append((ch["y"].astype(BF16), head_ones))
        means = yield requests
        for ch, mean in zip(chains, means):
            ch["d"] = ch["y"] - mean * (1.0 / HEAD_DIM)
        variances = yield [((ch["d"] * ch["d"]).astype(BF16), head_ones) for ch in chains]
        for ch, var in zip(chains, variances):
            sl = ch["sl"]
            y = ch["d"] * lax.rsqrt(var * (1.0 / HEAD_DIM) + GN_EPS) * gng_ref[:, sl] + gnb_ref[:, sl]
            y = y + ch["bonus"] * ch["v"]
            y = y * _silu(pr_ref[b, :, GATE_OFF + sl.start:GATE_OFF + sl.stop])
            y_ref[b, :, sl] = y.astype(y_ref.dtype)

    programs = [sequence(b) for b in range(nb)]
    requests = [next(prog) for prog in programs]
    while programs:
        products = [[None] * len(reqs) for reqs in requests]
        by_right = {}
        for j, reqs in enumerate(requests):
            for i, (left, right) in enumerate(reqs):
                by_right.setdefault(id(right), (right, []))[1].append((j, i, left))
        for right, users in by_right.values():
            tall = _mm(jnp.concatenate([left for _, _, left in users], axis=0), right)
            row = 0
            for j, i, left in users:
                products[j][i] = tall[row:row + left.shape[0]]
                row += left.shape[0]
        alive, requests = [], []
        for prog, prods in zip(programs, products):
            try:
                requests.append(prog.send(prods))
                alive.append(prog)
            except StopIteration:
                pass
        programs = alive


def _mm(a, b):
    return jnp.dot(a, b, preferred_element_type=F32)


def _mm_nt(a, b):
    return lax.dot_general(a, b, (((1,), (1,)), ((), ())), preferred_element_type=F32)


def _rwkv(pr, vfirst, mu, w0, w2, a0, a2, kks, ka, rk, gng, gnb, vres, bsz, tlen):
    nc = tlen // CHUNK
    nb = RWKV_SEQS if bsz % RWKV_SEQS == 0 else 1
    has_vres = vres is not None
    full = lambda a: pl.BlockSpec(a.shape, lambda b, t: (0,) * a.ndim)
    row_spec = lambda width: pl.BlockSpec((nb, CHUNK, width), lambda b, t: (b, t, 0))
    params = [mu, w0, w2, a0, a2, kks, ka, rk, gng, gnb]
    in_specs = [row_spec(R_IN)]
    args = [pr]
    if has_vres:
        in_specs.append(row_spec(RWKV_WIDTH))
        args.append(vfirst)
        params = params + list(vres)
    in_specs += [full(a) for a in params]
    args += params
    y_shape = jax.ShapeDtypeStruct((bsz, tlen, RWKV_WIDTH), BF16)
    if has_vres:
        out_specs, out_shape = row_spec(RWKV_WIDTH), y_shape
    else:
        vf_shape = jax.ShapeDtypeStruct((bsz, tlen, RWKV_WIDTH), F32)
        out_specs, out_shape = [row_spec(RWKV_WIDTH), row_spec(RWKV_WIDTH)], [y_shape, vf_shape]
    return pl.pallas_call(
        functools.partial(_rwkv_kernel, has_vres=has_vres),
        grid=(bsz // nb, nc),
        in_specs=in_specs,
        out_specs=out_specs,
        out_shape=out_shape,
        scratch_shapes=[pltpu.VMEM((nb, SUBLANES, SHIFT_COLS), F32),
                        pltpu.VMEM((nb * PAIRS, LANES, LANES), F32)],
        compiler_params=pltpu.CompilerParams(dimension_semantics=("parallel", "arbitrary"),
                                             vmem_limit_bytes=VMEM_LIMIT),
        name="rwkv7_vres" if has_vres else "rwkv7",
    )(*args)


OUT_SUB = 256


def _outproj_kernel(x_ref, yc_ref, ys_ref, yr_ref, w_ref, g_ref, o_ref):
    groups = [slice(r0, r0 + OUT_SUB) for r0 in range(0, x_ref.shape[0], OUT_SUB)]
    mixes = [jnp.dot(jnp.concatenate([yc_ref[rs, :], ys_ref[rs, :], yr_ref[rs, :]], axis=1), w_ref[...],
                     preferred_element_type=F32) for rs in groups]
    for rs, m in zip(groups, mixes):
        ms = jnp.mean(m * m, axis=-1, keepdims=True)
        o_ref[rs, :] = x_ref[rs, :] + m * lax.rsqrt(ms + RMS_EPS) * g_ref[...]


def _outproj(x2, yc, ys, yr, w, layer, g, tm):
    n = x2.shape[0]
    full = lambda a: pl.BlockSpec(a.shape, lambda i: (0, 0))
    rows = lambda a: pl.BlockSpec((tm, a.shape[1]), lambda i: (i, 0))
    return pl.pallas_call(
        _outproj_kernel,
        grid=(n // tm,),
        in_specs=[rows(x2), rows(yc), rows(ys), rows(yr), _layer_weight(w, layer), full(g)],
        out_specs=rows(x2),
        out_shape=jax.ShapeDtypeStruct(x2.shape, F32),
        compiler_params=pltpu.CompilerParams(dimension_semantics=("parallel",),
                                             vmem_limit_bytes=VMEM_LIMIT),
        name="outproj",
    )(x2, yc, ys, yr, w, g)


def _pad_rows(a, rows):
    return jnp.concatenate([a, jnp.zeros((rows - a.shape[0],) + a.shape[1:], a.dtype)], axis=0)


def kernel(x, pre_norm_g, post_norm_g, w_in, w_out, conv_dw, conv_dw_b, conv_ln_g, conv_ln_b, conv_pw, conv_pw_b, rwkv_mu, rwkv_w0, rwkv_w2, rwkv_a0, rwkv_a2, rwkv_kk_scale, rwkv_ka, rwkv_rk, rwkv_gn_g, rwkv_gn_b, rwkv_v0, rwkv_v1, rwkv_v2):
    bsz, tlen, dm = x.shape
    depth = w_in.shape[0]
    assert dm == D_MODEL and tlen % BLOCK_Q == 0 and tlen % CHUNK == 0
    n = bsz * tlen
    tm = 256 if tlen % 256 == 0 else BLOCK_Q
    tm_out = next(g * OUT_SUB for g in (4, 2, 1) if n % (g * OUT_SUB) == 0)
    row = lambda a: a.reshape(1, -1)
    assert w_in.shape[2] == D_IN
    w_in_b, w_out_b = w_in.astype(BF16), w_out.astype(BF16)
    x2 = x.reshape(n, dm)
    vfirst = None
    for l in range(depth):
        conv_params = (conv_dw[l], row(conv_dw_b[l]), row(conv_ln_g[l]), row(conv_ln_b[l]),
                       conv_pw[l].astype(BF16), row(conv_pw_b[l]))
        yc, qkv, g_sb, pr = _inproj(x2, row(pre_norm_g[l]), w_in_b, l, conv_params, tm, tlen)
        ys = _sb(qkv, g_sb, bsz, tlen)

        mu = jnp.concatenate([rwkv_mu[l], jnp.zeros((WA_WIDTH - 2 * LORA,), F32)]).reshape(1, -1)
        w2 = _pad_rows(rwkv_w2[l], WA_WIDTH).astype(BF16)
        a2 = _pad_rows(jnp.concatenate([jnp.zeros((LORA, RWKV_WIDTH), F32), rwkv_a2[l]], axis=0),
                       WA_WIDTH).astype(BF16)
        vres = None
        if l > 0:
            v1 = jnp.concatenate([rwkv_v1[l - 1], jnp.zeros((RWKV_WIDTH, LANES - LORA), F32)], axis=1)
            vres = (row(rwkv_v0[l - 1]), v1.astype(BF16), _pad_rows(rwkv_v2[l - 1], LANES).astype(BF16))
        res = _rwkv(pr.reshape(bsz, tlen, R_IN), vfirst, mu, row(rwkv_w0[l]), w2, row(rwkv_a0[l]), a2,
                    row(rwkv_kk_scale[l]), row(rwkv_ka[l]), row(rwkv_rk[l]), row(rwkv_gn_g[l]),
                    row(rwkv_gn_b[l]), vres, bsz, tlen)
        if l == 0:
            yr, vfirst = res
        else:
            yr = res
        yr = yr.reshape(n, RWKV_WIDTH)

        x2 = _outproj(x2, yc, ys, yr, w_out_b, l, row(post_norm_g[l]), tm_out)
    return x2.reshape(bsz, tlen, dm)
```

```python
import functools
import math

import jax
import jax.numpy as jnp
from jax import lax
from jax.experimental import pallas as pl
from jax.experimental.pallas import tpu as pltpu

D_MODEL = 1024
HEAD_DIM = 64
CONV_WIDTH = 256
SB_WIDTH = 384
RWKV_WIDTH = 384
CONV_TAPS = 31
LORA = 32
BLOCK_Q = 128
RMS_EPS = 1e-6
LN_EPS = 1e-5
GN_EPS = 64e-5

LANES = 128
SUBLANES = 8
PAIRS = RWKV_WIDTH // LANES
CHUNK = 64
assert CHUNK == HEAD_DIM
RWKV_SEQS = 8
C_END = 3 * CONV_WIDTH
S_END = C_END + 4 * SB_WIDTH
R_IN = 4 * RWKV_WIDTH + 2 * LORA
D_IN = S_END + R_IN
GATE_OFF = 3 * RWKV_WIDTH + 2 * LORA
WA_WIDTH = LANES
SHIFT_COLS = 3 * RWKV_WIDTH + WA_WIDTH
VMEM_LIMIT = 56 * 1024 * 1024

F32 = jnp.float32
BF16 = jnp.bfloat16


def _dot1(a, b):
    return jnp.dot(a.astype(BF16), b.astype(BF16), preferred_element_type=F32)


def _dot_nt1(a, b):
    return lax.dot_general(a.astype(BF16), b.astype(BF16), (((1,), (1,)), ((), ())),
                           preferred_element_type=F32)


def _sigmoid(x):
    return 0.5 * jnp.tanh(0.5 * x) + 0.5


def _silu(x):
    return x * _sigmoid(x)


CONV_HALO = 32
CONV_SUB = 32
MXU_COLS = 256


def _proj_slabs(h, w_ref, o_ref, cols=None):
    lo, hi = cols if cols is not None else (0, w_ref.shape[1])

    def slab(c0, c1):
        o_ref[:, c0 - lo:c1 - lo] = jnp.dot(h, w_ref[:, c0:c1], preferred_element_type=F32).astype(o_ref.dtype)

    return [functools.partial(slab, c0, min(c0 + MXU_COLS, hi)) for c0 in range(lo, hi, MXU_COLS)]


def _inproj_kernel(x_ref, g_ref, w_ref, dw_ref, dwb_ref, lng_ref, lnb_ref, pw_ref, pwb_ref,
                   yc_ref, qkv_ref, gs_ref, pr_ref, pc_ref, u_ref, *, tiles_per_seq):
    tt = x_ref.shape[0]

    @pl.when(pl.program_id(0) % tiles_per_seq == 0)
    def _():
        u_ref[0:CONV_HALO, :] = jnp.zeros((CONV_HALO, CONV_WIDTH), F32)

    x = x_ref[...]
    ms = jnp.mean(x * x, axis=-1, keepdims=True)
    h = (x * lax.rsqrt(ms + RMS_EPS) * g_ref[...]).astype(BF16)
    for job in _proj_slabs(h, w_ref, pc_ref, (0, C_END)):
        job()
    u_ref[CONV_HALO:CONV_HALO + tt, :] = pc_ref[:, 0:CONV_WIDTH] * _sigmoid(pc_ref[:, CONV_WIDTH:2 * CONV_WIDTH])
    slabs = (_proj_slabs(h, w_ref, qkv_ref, (C_END, C_END + 3 * SB_WIDTH))
             + _proj_slabs(h, w_ref, gs_ref, (C_END + 3 * SB_WIDTH, S_END)) + _proj_slabs(h, w_ref, pr_ref, (S_END, D_IN)))
    groups = tt // CONV_SUB
    for r0 in range(0, tt, CONV_SUB):
        gi = r0 // CONV_SUB
        for job in slabs[gi * len(slabs) // groups:(gi + 1) * len(slabs) // groups]:
            job()
        halves = []
        for l0 in range(0, CONV_WIDTH, LANES):
            ls = slice(l0, l0 + LANES)
            window = u_ref[r0:r0 + CONV_HALO + CONV_SUB, ls]
            acc = jnp.broadcast_to(dwb_ref[:, ls], (CONV_SUB, LANES))
            for b in range(SUBLANES):
                part = None
                for a in range((CONV_TAPS - 1 - b) // SUBLANES + 1):
                    tap = CONV_TAPS - 1 - (SUBLANES * a + b)
                    s = CONV_HALO - SUBLANES * (a + 1)
                    term = dw_ref[tap:tap + 1, ls] * window[s:s + CONV_SUB + SUBLANES, :]
                    part = term if part is None else part + term
                acc = acc + part[SUBLANES - b:SUBLANES - b + CONV_SUB, :]
            halves.append(acc)
        acc = jnp.concatenate(halves, axis=1)
        mean = jnp.mean(acc, axis=-1, keepdims=True)
        d = acc - mean
        var = jnp.mean(d * d, axis=-1, keepdims=True)
        y = _silu(d * lax.rsqrt(var + LN_EPS) * lng_ref[...] + lnb_ref[...])
        y = _dot1(y, pw_ref[...]) + pwb_ref[...]
        y = y * _silu(pc_ref[r0:r0 + CONV_SUB, 2 * CONV_WIDTH:3 * CONV_WIDTH])
        yc_ref[r0:r0 + CONV_SUB, :] = y.astype(yc_ref.dtype)
    u_ref[0:CONV_HALO, :] = u_ref[tt:tt + CONV_HALO, :]


def _layer_weight(w, layer):
    return pl.BlockSpec((None,) + w.shape[1:], lambda i: (layer, 0, 0))


def _inproj(x2, g, w, layer, conv_params, tm, tlen):
    n = x2.shape[0]
    full = lambda a: pl.BlockSpec(a.shape, lambda i: (0, 0))
    rows = lambda width: pl.BlockSpec((tm, width), lambda i: (i, 0))
    widths = (CONV_WIDTH, 3 * SB_WIDTH, SB_WIDTH, R_IN)
    dtypes = (BF16, BF16, F32, F32)
    return pl.pallas_call(
        functools.partial(_inproj_kernel, tiles_per_seq=tlen // tm),
        grid=(n // tm,),
        in_specs=[rows(D_MODEL), full(g), _layer_weight(w, layer)] + [full(a) for a in conv_params],
        out_specs=[rows(wd) for wd in widths],
        out_shape=[jax.ShapeDtypeStruct((n, wd), dt) for wd, dt in zip(widths, dtypes)],
        scratch_shapes=[pltpu.VMEM((tm, C_END), F32),
                        pltpu.VMEM((CONV_HALO + tm, CONV_WIDTH), F32)],
        compiler_params=pltpu.CompilerParams(dimension_semantics=("arbitrary",),
                                             vmem_limit_bytes=VMEM_LIMIT),
        name="inproj_conv",
    )(x2, g, w, *conv_params)


SB_DEAD_LOG = -150.0
SB_EAGER = 2


def _sb_kernel(q_ref, k_ref, v_ref, g_ref, o_ref, carry_ref, acc_ref):
    i = pl.program_id(1)
    bq = BLOCK_Q
    lane = lax.broadcasted_iota(jnp.int32, (bq, LANES), 1)
    head0 = lane < HEAD_DIM
    row = lax.broadcasted_iota(jnp.int32, (LANES, 2 * LANES), 0)
    col = lax.broadcasted_iota(jnp.int32, (LANES, 2 * LANES), 1)
    cs = jnp.where((col >= LANES) | (row > col), -1.0, 0.0).astype(BF16)
    pairs = range(PAIRS)
    sls = [slice(p * LANES, (p + 1) * LANES) for p in pairs]

    def stacked_q(p):
        q = q_ref[:, sls[p]].astype(F32) * (HEAD_DIM ** -0.5)
        return jnp.concatenate([jnp.where(head0, q, 0.0), jnp.where(head0, 0.0, q)], axis=0).astype(BF16)

    def process(js, first):
        starts = [pl.multiple_of(j * bq, bq) for j in js]
        items = [(n, p) for n in range(len(js)) for p in pairs]
        qs = [stacked_q(p) for p in pairs]
        zs = {(n, p): _mm_nt(qs[p], k_ref[pl.ds(starts[n], bq), sls[p]]) for n, p in items}
        if first:
            qi = lax.broadcasted_iota(jnp.int32, (2 * bq, bq), 0) % bq
            ki = lax.broadcasted_iota(jnp.int32, (2 * bq, bq), 1)
            causal = ki < qi
        log_betas, sums = {}, {}
        for n, p in items:
            z = zs[n, p]
            neg_log_keep = jnp.maximum(z, 0.0) + jnp.log(1.0 + jnp.exp(-jnp.abs(z)))
            log_betas[n, p] = z - neg_log_keep
            if first and n == 0:
                neg_log_keep = jnp.where(causal, neg_log_keep, 0.0)
            sums[n, p] = _dot1(neg_log_keep, cs)
        live = None
        pvs = {}
        for p in pairs:
            carry = None if first else carry_ref[p]
            for n in range(len(js)):
                rest = sums[n, p][:, :LANES]
                if carry is not None:
                    rest = rest + carry
                att = jnp.exp(log_betas[n, p] + rest)
                if first and n == 0:
                    att = jnp.where(causal, att, 0.0)
                row_sum = sums[n, p][:, LANES:]
                carry = row_sum if carry is None else carry + row_sum
                pvs[n, p] = _dot1(att, v_ref[pl.ds(starts[n], bq), sls[p]])
            carry_ref[p] = carry
            m = jnp.max(carry)
            live = m if live is None else jnp.maximum(live, m)
        for p in pairs:
            tot = pvs[0, p]
            for n in range(1, len(js)):
                tot = tot + pvs[n, p]
            if first:
                acc_ref[p] = tot
            else:
                acc_ref[p] += tot
        return live

    def eager(_):
        return jnp.int32(SB_EAGER), process([i - n for n in range(SB_EAGER + 1)], True)

    def diagonal_only(_):
        return jnp.int32(0), process([i], True)

    done, live = lax.cond(i >= SB_EAGER, eager, diagonal_only, None)

    def cond(c):
        return (c[0] < i) & (c[1] > SB_DEAD_LOG)

    def body(c):
        return c[0] + 1, process([i - 1 - c[0]], False)

    lax.while_loop(cond, body, (done, live))
    for p in range(PAIRS):
        sl = slice(p * LANES, (p + 1) * LANES)
        o = jnp.where(head0, acc_ref[p, :bq], acc_ref[p, bq:])
        o_ref[:, sl] = (o * _silu(g_ref[:, sl])).astype(o_ref.dtype)


def _sb(qkv, gate, bsz, tlen):
    nq = tlen // BLOCK_Q
    return pl.pallas_call(
        _sb_kernel,
        grid=(bsz, nq),
        in_specs=[pl.BlockSpec((BLOCK_Q, SB_WIDTH), lambda b, i: (b * nq + i, 0)),
                  pl.BlockSpec((tlen, SB_WIDTH), lambda b, i: (b, 1)),
                  pl.BlockSpec((tlen, SB_WIDTH), lambda b, i: (b, 2)),
                  pl.BlockSpec((BLOCK_Q, SB_WIDTH), lambda b, i: (b * nq + i, 0))],
        out_specs=pl.BlockSpec((BLOCK_Q, SB_WIDTH), lambda b, i: (b * nq + i, 0)),
        out_shape=jax.ShapeDtypeStruct((bsz * tlen, SB_WIDTH), BF16),
        scratch_shapes=[pltpu.VMEM((PAIRS, 2 * BLOCK_Q, LANES), F32),
                        pltpu.VMEM((PAIRS, 2 * BLOCK_Q, LANES), F32)],
        compiler_params=pltpu.CompilerParams(dimension_semantics=("parallel", "arbitrary"),
                                             vmem_limit_bytes=VMEM_LIMIT),
        name="stickbreak",
    )(qkv, qkv, qkv, gate)


def _stack_heads(a, head0):
    return jnp.concatenate([jnp.where(head0, a, 0.0), jnp.where(head0, 0.0, a)], axis=0)


def _rwkv_kernel(*refs, has_vres):
    if has_vres:
        (pr_ref, vf_ref, mu_ref, w0_ref, w2_ref, a0_ref, a2_ref, kks_ref, ka_ref, rk_ref, gng_ref, gnb_ref,
         v0_ref, v1_ref, v2_ref, y_ref, xs_ref, st_ref) = refs
    else:
        (pr_ref, mu_ref, w0_ref, w2_ref, a0_ref, a2_ref, kks_ref, ka_ref, rk_ref, gng_ref, gnb_ref,
         y_ref, vf_out_ref, xs_ref, st_ref) = refs
    c = CHUNK
    w = RWKV_WIDTH
    nb = pr_ref.shape[0]

    @pl.when(pl.program_id(1) == 0)
    def _():
        xs_ref[...] = jnp.zeros(xs_ref.shape, F32)
        st_ref[...] = jnp.zeros(st_ref.shape, F32)

    lane = lax.broadcasted_iota(jnp.int32, (c, LANES), 1)
    head0 = lane < HEAD_DIM
    ri = lax.broadcasted_iota(jnp.int32, (LANES, LANES), 0)
    ci = lax.broadcasted_iota(jnp.int32, (LANES, LANES), 1)
    same_head = (ri < HEAD_DIM) == (ci < HEAD_DIM)
    head_ones = jnp.where(same_head, 1.0, 0.0).astype(BF16)
    weights = dict(w2=w2_ref[...], a2=a2_ref[...])
    if has_vres:
        weights.update(v1=v1_ref[...], v2=v2_ref[...])
    ti = lax.broadcasted_iota(jnp.int32, (c, LANES), 0)
    ji = lane % HEAD_DIM
    lower = ti > ji
    lower_eq = ti >= ji
    eye = jnp.where(ti == ji, 1.0, 0.0)
    rc = lax.broadcasted_iota(jnp.int32, (c, c), 0)
    cc = lax.broadcasted_iota(jnp.int32, (c, c), 1)
    cum = jnp.where(rc >= cc, 1.0, 0.0).astype(BF16)

    def sequence(b):
        x = pr_ref[b, :, 0:SHIFT_COLS]
        first_row = lax.broadcasted_iota(jnp.int32, (c, SHIFT_COLS), 0) == 0
        prev = jnp.where(first_row, xs_ref[b, 0:1, :], pltpu.roll(x, 1, axis=0))
        xs_ref[b, 0:1, :] = x[c - 1:c, :]
        f = x + (prev - x) * mu_ref[...]
        r = f[:, 0:w]
        k = f[:, w:2 * w]
        v = f[:, 2 * w:3 * w]
        wa = f[:, 3 * w:3 * w + WA_WIDTH]
        lora = [(jnp.tanh(wa).astype(BF16), weights["w2"]), (wa.astype(BF16), weights["a2"])]
        if has_vres:
            lora.append((v.astype(BF16), weights["v1"]))
        products = yield lora
        logw = (-math.exp(-0.5)) * _sigmoid(w0_ref[...] + products[0])
        a_lr = _sigmoid(a0_ref[...] + products[1])
        logw_mxu = logw.astype(BF16)
        logp = _mm(cum, logw_mxu)
        logw = logw_mxu.astype(F32)
        if has_vres:
            (v_up,) = yield [(products[2].astype(BF16), weights["v2"])]
            v = v + (vf_ref[b] - v) * _sigmoid(v0_ref[...] + v_up)
        else:
            vf_out_ref[b] = v
            yield []
        p = jnp.exp(logp)
        p_prev = jnp.exp(logp - logw)
        inv_p = jnp.exp(-logp)
        kk = k * kks_ref[...]
        k2 = k * (1.0 + (a_lr - 1.0) * ka_ref[...])
        sls = [slice(pr * LANES, (pr + 1) * LANES) for pr in range(PAIRS)]
        n2s = yield [((kk[:, sl] * kk[:, sl]).astype(BF16), head_ones) for sl in sls]
        stack = lambda a: _stack_heads(a, head0).astype(BF16)
        chains = []
        for pr, sl in enumerate(sls):
            kk_p = kk[:, sl] * lax.rsqrt(jnp.maximum(n2s[pr], 1e-24))
            a_n = -kk_p * p_prev[:, sl]
            b_n = kk_p * a_lr[:, sl] * inv_p[:, sl]
            k_n = k2[:, sl] * inv_p[:, sl]
            r_n = r[:, sl] * p[:, sl]
            pe = p[c - 1:c, sl]
            ar = jnp.concatenate([a_n, r_n], axis=0).astype(BF16)
            ch = dict(sl=sl, idx=b * PAIRS + pr, ar=ar, pe=pe, v_st=stack(v[:, sl]),
                      be_ke=jnp.concatenate([(b_n * pe).T, (k_n * pe).T], axis=1).astype(BF16),
                      rkk=(r[:, sl] * k2[:, sl] * rk_ref[:, sl]).astype(BF16), v=v[:, sl])
            ch["gram"] = _mm_nt(ar, jnp.concatenate([stack(b_n), stack(k_n)], axis=0))
            chains.append(ch)
        yield []
        requests = []
        for ch in chains:
            gram = ch.pop("gram")
            ab = jnp.where(lower, gram[:c, :LANES], 0.0)
            ak = jnp.where(lower, gram[:c, LANES:], 0.0).astype(BF16)
            ch["rb_rk"] = jnp.concatenate([jnp.where(lower_eq, gram[c:, :LANES], 0.0),
                                           jnp.where(lower_eq, gram[c:, LANES:], 0.0)], axis=1).astype(BF16)
            g = st_ref[ch["idx"]]
            ch["g"] = g
            ch["s"] = eye + ab
            requests += [(ak, ch["v_st"]), (ch["ar"], g.astype(BF16)), (ab.astype(BF16), stack(ab))]
        products = yield requests
        for i, ch in enumerate(chains):
            ch["akv"], ch["ar_g"], ch["pw"] = products[3 * i:3 * i + 3]
        for _ in range(4):
            for ch in chains:
                res = _mm(ch["pw"].astype(BF16), jnp.concatenate([stack(ch["s"]), stack(ch["pw"])], axis=1))
                ch["s"] = ch["s"] + res[:, :LANES]
                ch["pw"] = res[:, LANES:]
            yield []
        products = yield [(ch["pw"].astype(BF16), stack(ch["s"])) for ch in chains]
        for ch, prod in zip(chains, products):
            ch["s"] = ch["s"] + prod
        products = yield [(ch["s"].astype(BF16), stack(ch["ar_g"][:c] + ch["akv"])) for ch in chains]
        requests = []
        for ch, u in zip(chains, products):
            ch["y_in"] = _mm(ch["rb_rk"], jnp.concatenate([stack(u), ch["v_st"]], axis=0))
            requests += [(ch["be_ke"], jnp.concatenate([u.astype(BF16), ch["v"].astype(BF16)], axis=0)),
                         (ch["rkk"], head_ones)]
        products = yield requests
        requests = []
        for i, ch in enumerate(chains):
            inc, ch["bonus"] = products[2 * i:2 * i + 2]
            pcol = jnp.broadcast_to(ch["pe"], (LANES, LANES)).T
            st_ref[ch["idx"]] = pcol * ch["g"] + jnp.where(same_head, inc, 0.0)
            ch["y"] = ch["ar_g"][c:] + ch["y_in"]
            requests.append((ch["y"].astype(BF16), head_ones))
        means = yield requests
        for ch, mean in zip(chains, means):
            ch["d"] = ch["y"] - mean * (1.0 / HEAD_DIM)
        variances = yield [((ch["d"] * ch["d"]).astype(BF16), head_ones) for ch in chains]
        for ch, var in zip(chains, variances):
            sl = ch["sl"]
            y = ch["d"] * lax.rsqrt(var * (1.0 / HEAD_DIM) + GN_EPS) * gng_ref[:, sl] + gnb_ref[:, sl]
            y = y + ch["bonus"] * ch["v"]
            y = y * _silu(pr_ref[b, :, GATE_OFF + sl.start:GATE_OFF + sl.stop])
            y_ref[b, :, sl] = y.astype(y_ref.dtype)

    programs = [sequence(b) for b in range(nb)]
    requests = [next(prog) for prog in programs]
    while programs:
        products = [[None] * len(reqs) for reqs in requests]
        by_right = {}
        for j, reqs in enumerate(requests):
            for i, (left, right) in enumerate(reqs):
                by_right.setdefault(id(right), (right, []))[1].append((j, i, left))
        for right, users in by_right.values():
            tall = _mm(jnp.concatenate([left for _, _, left in users], axis=0), right)
            row = 0
            for j, i, left in users:
                products[j][i] = tall[row:row + left.shape[0]]
                row += left.shape[0]
        alive, requests = [], []
        for prog, prods in zip(programs, products):
            try:
                requests.append(prog.send(prods))
                alive.append(prog)
            except StopIteration:
                pass
        programs = alive


def _mm(a, b):
    return jnp.dot(a, b, preferred_element_type=F32)


def _mm_nt(a, b):
    return lax.dot_general(a, b, (((1,), (1,)), ((), ())), preferred_element_type=F32)


def _rwkv(pr, vfirst, mu, w0, w2, a0, a2, kks, ka, rk, gng, gnb, vres, bsz, tlen):
    nc = tlen // CHUNK
    nb = RWKV_SEQS if bsz % RWKV_SEQS == 0 else 1
    has_vres = vres is not None
    full = lambda a: pl.BlockSpec(a.shape, lambda b, t: (0,) * a.ndim)
    row_spec = lambda width: pl.BlockSpec((nb, CHUNK, width), lambda b, t: (b, t, 0))
    params = [mu, w0, w2, a0, a2, kks, ka, rk, gng, gnb]
    in_specs = [row_spec(R_IN)]
    args = [pr]
    if has_vres:
        in_specs.append(row_spec(RWKV_WIDTH))
        args.append(vfirst)
        params = params + list(vres)
    in_specs += [full(a) for a in params]
    args += params
    y_shape = jax.ShapeDtypeStruct((bsz, tlen, RWKV_WIDTH), BF16)
    if has_vres:
        out_specs, out_shape = row_spec(RWKV_WIDTH), y_shape
    else:
        vf_shape = jax.ShapeDtypeStruct((bsz, tlen, RWKV_WIDTH), F32)
        out_specs, out_shape = [row_spec(RWKV_WIDTH), row_spec(RWKV_WIDTH)], [y_shape, vf_shape]
    return pl.pallas_call(
        functools.partial(_rwkv_kernel, has_vres=has_vres),
        grid=(bsz // nb, nc),
        in_specs=in_specs,
        out_specs=out_specs,
        out_shape=out_shape,
        scratch_shapes=[pltpu.VMEM((nb, SUBLANES, SHIFT_COLS), F32),
                        pltpu.VMEM((nb * PAIRS, LANES, LANES), F32)],
        compiler_params=pltpu.CompilerParams(dimension_semantics=("parallel", "arbitrary"),
                                             vmem_limit_bytes=VMEM_LIMIT),
        name="rwkv7_vres" if has_vres else "rwkv7",
    )(*args)


OUT_SUB = 256


def _outproj_kernel(x_ref, yc_ref, ys_ref, yr_ref, w_ref, g_ref, o_ref):
    groups = [slice(r0, r0 + OUT_SUB) for r0 in range(0, x_ref.shape[0], OUT_SUB)]
    mixes = [jnp.dot(jnp.concatenate([yc_ref[rs, :], ys_ref[rs, :], yr_ref[rs, :]], axis=1), w_ref[...],
                     preferred_element_type=F32) for rs in groups]
    for rs, m in zip(groups, mixes):
        ms = jnp.mean(m * m, axis=-1, keepdims=True)
        o_ref[rs, :] = x_ref[rs, :] + m * lax.rsqrt(ms + RMS_EPS) * g_ref[...]


def _outproj(x2, yc, ys, yr, w, layer, g, tm):
    n = x2.shape[0]
    full = lambda a: pl.BlockSpec(a.shape, lambda i: (0, 0))
    rows = lambda a: pl.BlockSpec((tm, a.shape[1]), lambda i: (i, 0))
    return pl.pallas_call(
        _outproj_kernel,
        grid=(n // tm,),
        in_specs=[rows(x2), rows(yc), rows(ys), rows(yr), _layer_weight(w, layer), full(g)],
        out_specs=rows(x2),
        out_shape=jax.ShapeDtypeStruct(x2.shape, F32),
        compiler_params=pltpu.CompilerParams(dimension_semantics=("parallel",),
                                             vmem_limit_bytes=VMEM_LIMIT),
        name="outproj",
    )(x2, yc, ys, yr, w, g)


def _pad_rows(a, rows):
    return jnp.concatenate([a, jnp.zeros((rows - a.shape[0],) + a.shape[1:], a.dtype)], axis=0)


def kernel(x, pre_norm_g, post_norm_g, w_in, w_out, conv_dw, conv_dw_b, conv_ln_g, conv_ln_b, conv_pw, conv_pw_b, rwkv_mu, rwkv_w0, rwkv_w2, rwkv_a0, rwkv_a2, rwkv_kk_scale, rwkv_ka, rwkv_rk, rwkv_gn_g, rwkv_gn_b, rwkv_v0, rwkv_v1, rwkv_v2):
    bsz, tlen, dm = x.shape
    depth = w_in.shape[0]
    assert dm == D_MODEL and tlen % BLOCK_Q == 0 and tlen % CHUNK == 0
    n = bsz * tlen
    tm = 256 if tlen % 256 == 0 else BLOCK_Q
    tm_out = next(g * OUT_SUB for g in (4, 2, 1) if n % (g * OUT_SUB) == 0)
    row = lambda a: a.reshape(1, -1)
    assert w_in.shape[2] == D_IN
    w_in_b, w_out_b = w_in.astype(BF16), w_out.astype(BF16)
    x2 = x.reshape(n, dm)
    vfirst = None
    for l in range(depth):
        conv_params = (conv_dw[l], row(conv_dw_b[l]), row(conv_ln_g[l]), row(conv_ln_b[l]),
                       conv_pw[l].astype(BF16), row(conv_pw_b[l]))
        yc, qkv, g_sb, pr = _inproj(x2, row(pre_norm_g[l]), w_in_b, l, conv_params, tm, tlen)
        ys = _sb(qkv, g_sb, bsz, tlen)

        mu = jnp.concatenate([rwkv_mu[l], jnp.zeros((WA_WIDTH - 2 * LORA,), F32)]).reshape(1, -1)
        w2 = _pad_rows(rwkv_w2[l], WA_WIDTH).astype(BF16)
        a2 = _pad_rows(jnp.concatenate([jnp.zeros((LORA, RWKV_WIDTH), F32), rwkv_a2[l]], axis=0),
                       WA_WIDTH).astype(BF16)
        vres = None
        if l > 0:
            v1 = jnp.concatenate([rwkv_v1[l - 1], jnp.zeros((RWKV_WIDTH, LANES - LORA), F32)], axis=1)
            vres = (row(rwkv_v0[l - 1]), v1.astype(BF16), _pad_rows(rwkv_v2[l - 1], LANES).astype(BF16))
        res = _rwkv(pr.reshape(bsz, tlen, R_IN), vfirst, mu, row(rwkv_w0[l]), w2, row(rwkv_a0[l]), a2,
                    row(rwkv_kk_scale[l]), row(rwkv_ka[l]), row(rwkv_rk[l]), row(rwkv_gn_g[l]),
                    row(rwkv_gn_b[l]), vres, bsz, tlen)
        if l == 0:
            yr, vfirst = res
        else:
            yr = res
        yr = yr.reshape(n, RWKV_WIDTH)

        x2 = _outproj(x2, yc, ys, yr, w_out_b, l, row(post_norm_g[l]), tm_out)
    return x2.reshape(bsz, tlen, dm)
```

```python
import functools
import math

import jax
import jax.numpy as jnp
from jax import lax
from jax.experimental import pallas as pl
from jax.experimental.pallas import tpu as pltpu

D_MODEL = 1024
HEAD_DIM = 64
CONV_WIDTH = 256
SB_WIDTH = 384
RWKV_WIDTH = 384
CONV_TAPS = 31
LORA = 32
BLOCK_Q = 128
RMS_EPS = 1e-6
LN_EPS = 1e-5
GN_EPS = 64e-5

LANES = 128
SUBLANES = 8
PAIRS = RWKV_WIDTH // LANES
CHUNK = 64
assert CHUNK == HEAD_DIM
RWKV_SEQS = 8
C_END = 3 * CONV_WIDTH
S_END = C_END + 4 * SB_WIDTH
R_IN = 4 * RWKV_WIDTH + 2 * LORA
D_IN = S_END + R_IN
GATE_OFF = 3 * RWKV_WIDTH + 2 * LORA
WA_WIDTH = LANES
SHIFT_COLS = 3 * RWKV_WIDTH + WA_WIDTH
VMEM_LIMIT = 56 * 1024 * 1024

F32 = jnp.float32
BF16 = jnp.bfloat16


def _dot1(a, b):
    return jnp.dot(a.astype(BF16), b.astype(BF16), preferred_element_type=F32)


def _sigmoid(x):
    return 0.5 * jnp.tanh(0.5 * x) + 0.5


def _silu(x):
    return x * _sigmoid(x)


CONV_HALO = 32
CONV_SUB = 32
MXU_COLS = 256


def _proj_slabs(h, w_ref, o_ref, cols=None):
    lo, hi = cols if cols is not None else (0, w_ref.shape[1])

    def slab(c0, c1):
        o_ref[:, c0 - lo:c1 - lo] = jnp.dot(h, w_ref[:, c0:c1], preferred_element_type=F32).astype(o_ref.dtype)

    return [functools.partial(slab, c0, min(c0 + MXU_COLS, hi)) for c0 in range(lo, hi, MXU_COLS)]


def _inproj_kernel(x_ref, g_ref, w_ref, dw_ref, dwb_ref, lng_ref, lnb_ref, pw_ref, pwb_ref,
                   yc_ref, qkv_ref, gs_ref, pr_ref, pc_ref, u_ref, *, tiles_per_seq):
    tt = x_ref.shape[0]

    @pl.when(pl.program_id(0) % tiles_per_seq == 0)
    def _():
        u_ref[0:CONV_HALO, :] = jnp.zeros((CONV_HALO, CONV_WIDTH), F32)

    x = x_ref[...]
    ms = jnp.mean(x * x, axis=-1, keepdims=True)
    h = (x * lax.rsqrt(ms + RMS_EPS) * g_ref[...]).astype(BF16)
    for job in _proj_slabs(h, w_ref, pc_ref, (0, C_END)):
        job()
    u_ref[CONV_HALO:CONV_HALO + tt, :] = pc_ref[:, 0:CONV_WIDTH] * _sigmoid(pc_ref[:, CONV_WIDTH:2 * CONV_WIDTH])
    slabs = (_proj_slabs(h, w_ref, qkv_ref, (C_END, C_END + 3 * SB_WIDTH))
             + _proj_slabs(h, w_ref, gs_ref, (C_END + 3 * SB_WIDTH, S_END)) + _proj_slabs(h, w_ref, pr_ref, (S_END, D_IN)))
    groups = tt // CONV_SUB
    for r0 in range(0, tt, CONV_SUB):
        gi = r0 // CONV_SUB
        for job in slabs[gi * len(slabs) // groups:(gi + 1) * len(slabs) // groups]:
            job()
        halves = []
        for l0 in range(0, CONV_WIDTH, LANES):
            ls = slice(l0, l0 + LANES)
            window = u_ref[r0:r0 + CONV_HALO + CONV_SUB, ls]
            acc = jnp.broadcast_to(dwb_ref[:, ls], (CONV_SUB, LANES))
            for b in range(SUBLANES):
                part = None
                for a in range((CONV_TAPS - 1 - b) // SUBLANES + 1):
                    tap = CONV_TAPS - 1 - (SUBLANES * a + b)
                    s = CONV_HALO - SUBLANES * (a + 1)
                    term = dw_ref[tap:tap + 1, ls] * window[s:s + CONV_SUB + SUBLANES, :]
                    part = term if part is None else part + term
                acc = acc + part[SUBLANES - b:SUBLANES - b + CONV_SUB, :]
            halves.append(acc)
        acc = jnp.concatenate(halves, axis=1)
        mean = jnp.mean(acc, axis=-1, keepdims=True)
        d = acc - mean
        var = jnp.mean(d * d, axis=-1, keepdims=True)
        y = _silu(d * lax.rsqrt(var + LN_EPS) * lng_ref[...] + lnb_ref[...])
        y = _dot1(y, pw_ref[...]) + pwb_ref[...]
        y = y * _silu(pc_ref[r0:r0 + CONV_SUB, 2 * CONV_WIDTH:3 * CONV_WIDTH])
        yc_ref[r0:r0 + CONV_SUB, :] = y.astype(yc_ref.dtype)
    u_ref[0:CONV_HALO, :] = u_ref[tt:tt + CONV_HALO, :]


def _layer_weight(w, layer):
    return pl.BlockSpec((None,) + w.shape[1:], lambda i: (layer, 0, 0))


def _inproj(x2, g, w, layer, conv_params, tm, tlen):
    n = x2.shape[0]
    full = lambda a: pl.BlockSpec(a.shape, lambda i: (0, 0))
    rows = lambda width: pl.BlockSpec((tm, width), lambda i: (i, 0))
    widths = (CONV_WIDTH, 3 * SB_WIDTH, SB_WIDTH, R_IN)
    dtypes = (BF16, BF16, F32, F32)
    return pl.pallas_call(
        functools.partial(_inproj_kernel, tiles_per_seq=tlen // tm),
        grid=(n // tm,),
        in_specs=[rows(D_MODEL), full(g), _layer_weight(w, layer)] + [full(a) for a in conv_params],
        out_specs=[rows(wd) for wd in widths],
        out_shape=[jax.ShapeDtypeStruct((n, wd), dt) for wd, dt in zip(widths, dtypes)],
        scratch_shapes=[pltpu.VMEM((tm, C_END), F32),
                        pltpu.VMEM((CONV_HALO + tm, CONV_WIDTH), F32)],
        compiler_params=pltpu.CompilerParams(dimension_semantics=("arbitrary",),
                                             vmem_limit_bytes=VMEM_LIMIT),
        name="inproj_conv",
    )(x2, g, w, *conv_params)


SB_DEAD_LOG = -150.0
SB_EAGER = 2


def _sb_kernel(q_ref, k_ref, v_ref, g_ref, o_ref, carry_ref, acc_ref):
    i = pl.program_id(1)
    bq = BLOCK_Q
    lane = lax.broadcasted_iota(jnp.int32, (bq, LANES), 1)
    head0 = lane < HEAD_DIM
    row = lax.broadcasted_iota(jnp.int32, (LANES, 2 * LANES), 0)
    col = lax.broadcasted_iota(jnp.int32, (LANES, 2 * LANES), 1)
    cs = jnp.where((col >= LANES) | (row > col), -1.0, 0.0).astype(BF16)
    pairs = range(PAIRS)
    sls = [slice(p * LANES, (p + 1) * LANES) for p in pairs]

    def stacked_q(p):
        q = q_ref[:, sls[p]].astype(F32) * (HEAD_DIM ** -0.5)
        return jnp.concatenate([jnp.where(head0, q, 0.0), jnp.where(head0, 0.0, q)], axis=0).astype(BF16)

    def process(js, first):
        starts = [pl.multiple_of(j * bq, bq) for j in js]
        items = [(n, p) for n in range(len(js)) for p in pairs]
        qs = [stacked_q(p) for p in pairs]
        zs = {(n, p): _mm_nt(qs[p], k_ref[pl.ds(starts[n], bq), sls[p]]) for n, p in items}
        if first:
            qi = lax.broadcasted_iota(jnp.int32, (2 * bq, bq), 0) % bq
            ki = lax.broadcasted_iota(jnp.int32, (2 * bq, bq), 1)
            causal = ki < qi
        log_betas, sums = {}, {}
        for n, p in items:
            z = zs[n, p]
            neg_log_keep = jnp.maximum(z, 0.0) + jnp.log(1.0 + jnp.exp(-jnp.abs(z)))
            log_betas[n, p] = z - neg_log_keep
            if first and n == 0:
                neg_log_keep = jnp.where(causal, neg_log_keep, 0.0)
            sums[n, p] = _dot1(neg_log_keep, cs)
        live = None
        pvs = {}
        for p in pairs:
            carry = None if first else carry_ref[p]
            for n in range(len(js)):
                rest = sums[n, p][:, :LANES]
                if carry is not None:
                    rest = rest + carry
                att = jnp.exp(log_betas[n, p] + rest)
                if first and n == 0:
                    att = jnp.where(causal, att, 0.0)
                row_sum = sums[n, p][:, LANES:]
                carry = row_sum if carry is None else carry + row_sum
                pvs[n, p] = _dot1(att, v_ref[pl.ds(starts[n], bq), sls[p]])
            carry_ref[p] = carry
            m = jnp.max(carry)
            live = m if live is None else jnp.maximum(live, m)
        for p in pairs:
            tot = pvs[0, p]
            for n in range(1, len(js)):
                tot = tot + pvs[n, p]
            if first:
                acc_ref[p] = tot
            else:
                acc_ref[p] += tot
        return live

    def eager(_):
        return jnp.int32(SB_EAGER), process([i - n for n in range(SB_EAGER + 1)], True)

    def diagonal_only(_):
        return jnp.int32(0), process([i], True)

    done, live = lax.cond(i >= SB_EAGER, eager, diagonal_only, None)

    def cond(c):
        return (c[0] < i) & (c[1] > SB_DEAD_LOG)

    def body(c):
        return c[0] + 1, process([i - 1 - c[0]], False)

    lax.while_loop(cond, body, (done, live))
    for p in range(PAIRS):
        sl = slice(p * LANES, (p + 1) * LANES)
        o = jnp.where(head0, acc_ref[p, :bq], acc_ref[p, bq:])
        o_ref[:, sl] = (o * _silu(g_ref[:, sl])).astype(o_ref.dtype)


def _sb(qkv, gate, bsz, tlen):
    nq = tlen // BLOCK_Q
    return pl.pallas_call(
        _sb_kernel,
        grid=(bsz, nq),
        in_specs=[pl.BlockSpec((BLOCK_Q, SB_WIDTH), lambda b, i: (b * nq + i, 0)),
                  pl.BlockSpec((tlen, SB_WIDTH), lambda b, i: (b, 1)),
                  pl.BlockSpec((tlen, SB_WIDTH), lambda b, i: (b, 2)),
                  pl.BlockSpec((BLOCK_Q, SB_WIDTH), lambda b, i: (b * nq + i, 0))],
        out_specs=pl.BlockSpec((BLOCK_Q, SB_WIDTH), lambda b, i: (b * nq + i, 0)),
        out_shape=jax.ShapeDtypeStruct((bsz * tlen, SB_WIDTH), BF16),
        scratch_shapes=[pltpu.VMEM((PAIRS, 2 * BLOCK_Q, LANES), F32),
                        pltpu.VMEM((PAIRS, 2 * BLOCK_Q, LANES), F32)],
        compiler_params=pltpu.CompilerParams(dimension_semantics=("parallel", "arbitrary"),
                                             vmem_limit_bytes=VMEM_LIMIT),
        name="stickbreak",
    )(qkv, qkv, qkv, gate)


def _stack_heads(a, head0):
    return jnp.concatenate([jnp.where(head0, a, 0.0), jnp.where(head0, 0.0, a)], axis=0)


def _rwkv_kernel(*refs, has_vres):
    if has_vres:
        (pr_ref, vf_ref, mu_ref, w0_ref, w2_ref, a0_ref, a2_ref, kks_ref, ka_ref, rk_ref, gng_ref, gnb_ref,
         v0_ref, v1_ref, v2_ref, y_ref, xs_ref, st_ref) = refs
    else:
        (pr_ref, mu_ref, w0_ref, w2_ref, a0_ref, a2_ref, kks_ref, ka_ref, rk_ref, gng_ref, gnb_ref,
         y_ref, vf_out_ref, xs_ref, st_ref) = refs
    c = CHUNK
    w = RWKV_WIDTH
    nb = pr_ref.shape[0]

    @pl.when(pl.program_id(1) == 0)
    def _():
        xs_ref[...] = jnp.zeros(xs_ref.shape, F32)
        st_ref[...] = jnp.zeros(st_ref.shape, F32)

    lane = lax.broadcasted_iota(jnp.int32, (c, LANES), 1)
    head0 = lane < HEAD_DIM
    ri = lax.broadcasted_iota(jnp.int32, (LANES, LANES), 0)
    ci = lax.broadcasted_iota(jnp.int32, (LANES, LANES), 1)
    same_head = (ri < HEAD_DIM) == (ci < HEAD_DIM)
    head_ones = jnp.where(same_head, 1.0, 0.0).astype(BF16)
    weights = dict(w2=w2_ref[...], a2=a2_ref[...])
    if has_vres:
        weights.update(v1=v1_ref[...], v2=v2_ref[...])
    ti = lax.broadcasted_iota(jnp.int32, (c, LANES), 0)
    ji = lane % HEAD_DIM
    lower = ti > ji
    lower_eq = ti >= ji
    eye = jnp.where(ti == ji, 1.0, 0.0)
    rc = lax.broadcasted_iota(jnp.int32, (c, c), 0)
    cc = lax.broadcasted_iota(jnp.int32, (c, c), 1)
    cum = jnp.where(rc >= cc, 1.0, 0.0).astype(BF16)

    def sequence(b):
        x = pr_ref[b, :, 0:SHIFT_COLS]
        first_row = lax.broadcasted_iota(jnp.int32, (c, SHIFT_COLS), 0) == 0
        prev = jnp.where(first_row, xs_ref[b, 0:1, :], pltpu.roll(x, 1, axis=0))
        xs_ref[b, 0:1, :] = x[c - 1:c, :]
        f = x + (prev - x) * mu_ref[...]
        r = f[:, 0:w]
        k = f[:, w:2 * w]
        v = f[:, 2 * w:3 * w]
        wa = f[:, 3 * w:3 * w + WA_WIDTH]
        lora = [(jnp.tanh(wa).astype(BF16), weights["w2"]), (wa.astype(BF16), weights["a2"])]
        if has_vres:
            lora.append((v.astype(BF16), weights["v1"]))
        products = yield lora
        logw = (-math.exp(-0.5)) * _sigmoid(w0_ref[...] + products[0])
        a_lr = _sigmoid(a0_ref[...] + products[1])
        logw_mxu = logw.astype(BF16)
        logp = _mm(cum, logw_mxu)
        logw = logw_mxu.astype(F32)
        if has_vres:
            (v_up,) = yield [(products[2].astype(BF16), weights["v2"])]
            v = v + (vf_ref[b] - v) * _sigmoid(v0_ref[...] + v_up)
        else:
            vf_out_ref[b] = v
            yield []
        p = jnp.exp(logp)
        p_prev = jnp.exp(logp - logw)
        inv_p = jnp.exp(-logp)
        kk = k * kks_ref[...]
        k2 = k * (1.0 + (a_lr - 1.0) * ka_ref[...])
        sls = [slice(pr * LANES, (pr + 1) * LANES) for pr in range(PAIRS)]
        n2s = yield [((kk[:, sl] * kk[:, sl]).astype(BF16), head_ones) for sl in sls]
        stack = lambda a: _stack_heads(a, head0).astype(BF16)
        chains = []
        for pr, sl in enumerate(sls):
            kk_p = kk[:, sl] * lax.rsqrt(jnp.maximum(n2s[pr], 1e-24))
            a_n = -kk_p * p_prev[:, sl]
            b_n = kk_p * a_lr[:, sl] * inv_p[:, sl]
            k_n = k2[:, sl] * inv_p[:, sl]
            r_n = r[:, sl] * p[:, sl]
            pe = p[c - 1:c, sl]
            ar = jnp.concatenate([a_n, r_n], axis=0).astype(BF16)
            ch = dict(sl=sl, idx=b * PAIRS + pr, ar=ar, pe=pe, v_st=stack(v[:, sl]),
                      be_ke=jnp.concatenate([(b_n * pe).T, (k_n * pe).T], axis=1).astype(BF16),
                      rkk=(r[:, sl] * k2[:, sl] * rk_ref[:, sl]).astype(BF16), v=v[:, sl])
            ch["gram"] = _mm_nt(ar, jnp.concatenate([stack(b_n), stack(k_n)], axis=0))
            chains.append(ch)
        yield []
        requests = []
        for ch in chains:
            gram = ch.pop("gram")
            ab = jnp.where(lower, gram[:c, :LANES], 0.0)
            ak = jnp.where(lower, gram[:c, LANES:], 0.0).astype(BF16)
            ch["rb_rk"] = jnp.concatenate([jnp.where(lower_eq, gram[c:, :LANES], 0.0),
                                           jnp.where(lower_eq, gram[c:, LANES:], 0.0)], axis=1).astype(BF16)
            g = st_ref[ch["idx"]]
            ch["g"] = g
            ch["s"] = eye + ab
            requests += [(ak, ch["v_st"]), (ch["ar"], g.astype(BF16)), (ab.astype(BF16), stack(ab))]
        products = yield requests
        for i, ch in enumerate(chains):
            ch["akv"], ch["ar_g"], ch["pw"] = products[3 * i:3 * i + 3]
        for _ in range(4):
            for ch in chains:
                res = _mm(ch["pw"].astype(BF16), jnp.concatenate([stack(ch["s"]), stack(ch["pw"])], axis=1))
                ch["s"] = ch["s"] + res[:, :LANES]
                ch["pw"] = res[:, LANES:]
            yield []
        products = yield [(ch["pw"].astype(BF16), stack(ch["s"])) for ch in chains]
        for ch, prod in zip(chains, products):
            ch["s"] = ch["s"] + prod
        products = yield [(ch["s"].astype(BF16), stack(ch["ar_g"][:c] + ch["akv"])) for ch in chains]
        requests = []
        for ch, u in zip(chains, products):
            ch["y_in"] = _mm(ch["rb_rk"], jnp.concatenate([stack(u), ch["v_st"]], axis=0))
            requests += [(ch["be_ke"], jnp.concatenate([u.astype(BF16), ch["v"].astype(BF16)], axis=0)),
                         (ch["rkk"], head_ones)]
        products = yield requests
        requests = []
        for i, ch in enumerate(chains):
            inc, ch["bonus"] = products[2 * i:2 * i + 2]
            pcol = jnp.broadcast_to(ch["pe"], (LANES, LANES)).T
            st_ref[ch["idx"]] = pcol * ch["g"] + jnp.where(same_head, inc, 0.0)
            ch["y"] = ch["ar_g"][c:] + ch["y_in"]
            requests.append((ch["y"].astype(BF16), head_ones))
        means = yield requests
        for ch, mean in zip(chains, means):
            ch["d"] = ch["y"] - mean * (1.0 / HEAD_DIM)
        variances = yield [((ch["d"] * ch["d"]).astype(BF16), head_ones) for ch in chains]
        for ch, var in zip(chains, variances):
            sl = ch["sl"]
            y = ch["d"] * lax.rsqrt(var * (1.0 / HEAD_DIM) + GN_EPS) * gng_ref[:, sl] + gnb_ref[:, sl]
            y = y + ch["bonus"] * ch["v"]
            y = y * _silu(pr_ref[b, :, GATE_OFF + sl.start:GATE_OFF + sl.stop])
            y_ref[b, :, sl] = y.astype(y_ref.dtype)

    programs = [sequence(b) for b in range(nb)]
    requests = [next(prog) for prog in programs]
    while programs:
        products = [[None] * len(reqs) for reqs in requests]
        by_right = {}
        for j, reqs in enumerate(requests):
            for i, (left, right) in enumerate(reqs):
                by_right.setdefault(id(right), (right, []))[1].append((j, i, left))
        for right, users in by_right.values():
            tall = _mm(jnp.concatenate([left for _, _, left in users], axis=0), right)
            row = 0
            for j, i, left in users:
                products[j][i] = tall[row:row + left.shape[0]]
                row += left.shape[0]
        alive, requests = [], []
        for prog, prods in zip(programs, products):
            try:
                requests.append(prog.send(prods))
                alive.append(prog)
            except StopIteration:
                pass
        programs = alive


def _mm(a, b):
    return jnp.dot(a, b, preferred_element_type=F32)


def _mm_nt(a, b):
    return lax.dot_general(a, b, (((1,), (1,)), ((), ())), preferred_element_type=F32)


def _rwkv(pr, vfirst, mu, w0, w2, a0, a2, kks, ka, rk, gng, gnb, vres, bsz, tlen):
    nc = tlen // CHUNK
    nb = RWKV_SEQS if bsz % RWKV_SEQS == 0 else 1
    has_vres = vres is not None
    full = lambda a: pl.BlockSpec(a.shape, lambda b, t: (0,) * a.ndim)
    row_spec = lambda width: pl.BlockSpec((nb, CHUNK, width), lambda b, t: (b, t, 0))
    params = [mu, w0, w2, a0, a2, kks, ka, rk, gng, gnb]
    in_specs = [row_spec(R_IN)]
    args = [pr]
    if has_vres:
        in_specs.append(row_spec(RWKV_WIDTH))
        args.append(vfirst)
        params = params + list(vres)
    in_specs += [full(a) for a in params]
    args += params
    y_shape = jax.ShapeDtypeStruct((bsz, tlen, RWKV_WIDTH), BF16)
    if has_vres:
        out_specs, out_shape = row_spec(RWKV_WIDTH), y_shape
    else:
        vf_shape = jax.ShapeDtypeStruct((bsz, tlen, RWKV_WIDTH), F32)
        out_specs, out_shape = [row_spec(RWKV_WIDTH), row_spec(RWKV_WIDTH)], [y_shape, vf_shape]
    return pl.pallas_call(
        functools.partial(_rwkv_kernel, has_vres=has_vres),
        grid=(bsz // nb, nc),
        in_specs=in_specs,
        out_specs=out_specs,
        out_shape=out_shape,
        scratch_shapes=[pltpu.VMEM((nb, SUBLANES, SHIFT_COLS), F32),
                        pltpu.VMEM((nb * PAIRS, LANES, LANES), F32)],
        compiler_params=pltpu.CompilerParams(dimension_semantics=("parallel", "arbitrary"),
                                             vmem_limit_bytes=VMEM_LIMIT),
        name="rwkv7_vres" if has_vres else "rwkv7",
    )(*args)


OUT_SUB = 256


def _outproj_kernel(x_ref, yc_ref, ys_ref, yr_ref, w_ref, g_ref, o_ref):
    groups = [slice(r0, r0 + OUT_SUB) for r0 in range(0, x_ref.shape[0], OUT_SUB)]
    mixes = [jnp.dot(jnp.concatenate([yc_ref[rs, :], ys_ref[rs, :], yr_ref[rs, :]], axis=1), w_ref[...],
                     preferred_element_type=F32) for rs in groups]
    for rs, m in zip(groups, mixes):
        ms = jnp.mean(m * m, axis=-1, keepdims=True)
        o_ref[rs, :] = x_ref[rs, :] + m * lax.rsqrt(ms + RMS_EPS) * g_ref[...]


def _outproj(x2, yc, ys, yr, w, layer, g, tm):
    n = x2.shape[0]
    full = lambda a: pl.BlockSpec(a.shape, lambda i: (0, 0))
    rows = lambda a: pl.BlockSpec((tm, a.shape[1]), lambda i: (i, 0))
    return pl.pallas_call(
        _outproj_kernel,
        grid=(n // tm,),
        in_specs=[rows(x2), rows(yc), rows(ys), rows(yr), _layer_weight(w, layer), full(g)],
        out_specs=rows(x2),
        out_shape=jax.ShapeDtypeStruct(x2.shape, F32),
        compiler_params=pltpu.CompilerParams(dimension_semantics=("parallel",),
                                             vmem_limit_bytes=VMEM_LIMIT),
        name="outproj",
    )(x2, yc, ys, yr, w, g)


def _pad_rows(a, rows):
    return jnp.concatenate([a, jnp.zeros((rows - a.shape[0],) + a.shape[1:], a.dtype)], axis=0)


def kernel(x, pre_norm_g, post_norm_g, w_in, w_out, conv_dw, conv_dw_b, conv_ln_g, conv_ln_b, conv_pw, conv_pw_b, rwkv_mu, rwkv_w0, rwkv_w2, rwkv_a0, rwkv_a2, rwkv_kk_scale, rwkv_ka, rwkv_rk, rwkv_gn_g, rwkv_gn_b, rwkv_v0, rwkv_v1, rwkv_v2):
    bsz, tlen, dm = x.shape
    depth = w_in.shape[0]
    assert dm == D_MODEL and tlen % BLOCK_Q == 0 and tlen % CHUNK == 0
    n = bsz * tlen
    tm = 256 if tlen % 256 == 0 else BLOCK_Q
    tm_out = next(g * OUT_SUB for g in (4, 2, 1) if n % (g * OUT_SUB) == 0)
    row = lambda a: a.reshape(1, -1)
    assert w_in.shape[2] == D_IN
    w_in_b, w_out_b = w_in.astype(BF16), w_out.astype(BF16)
    x2 = x.reshape(n, dm)
    vfirst = None
    for l in range(depth):
        conv_params = (conv_dw[l], row(conv_dw_b[l]), row(conv_ln_g[l]), row(conv_ln_b[l]),
                       conv_pw[l].astype(BF16), row(conv_pw_b[l]))
        yc, qkv, g_sb, pr = _inproj(x2, row(pre_norm_g[l]), w_in_b, l, conv_params, tm, tlen)
        ys = _sb(qkv, g_sb, bsz, tlen)

        mu = jnp.concatenate([rwkv_mu[l], jnp.zeros((WA_WIDTH - 2 * LORA,), F32)]).reshape(1, -1)
        w2 = _pad_rows(rwkv_w2[l], WA_WIDTH).astype(BF16)
        a2 = _pad_rows(jnp.concatenate([jnp.zeros((LORA, RWKV_WIDTH), F32), rwkv_a2[l]], axis=0),
                       WA_WIDTH).astype(BF16)
        vres = None
        if l > 0:
            v1 = jnp.concatenate([rwkv_v1[l - 1], jnp.zeros((RWKV_WIDTH, LANES - LORA), F32)], axis=1)
            vres = (row(rwkv_v0[l - 1]), v1.astype(BF16), _pad_rows(rwkv_v2[l - 1], LANES).astype(BF16))
        res = _rwkv(pr.reshape(bsz, tlen, R_IN), vfirst, mu, row(rwkv_w0[l]), w2, row(rwkv_a0[l]), a2,
                    row(rwkv_kk_scale[l]), row(rwkv_ka[l]), row(rwkv_rk[l]), row(rwkv_gn_g[l]),
                    row(rwkv_gn_b[l]), vres, bsz, tlen)
        if l == 0:
            yr, vfirst = res
        else:
            yr = res
        yr = yr.reshape(n, RWKV_WIDTH)

        x2 = _outproj(x2, yc, ys, yr, w_out_b, l, row(post_norm_g[l]), tm_out)
    return x2.reshape(bsz, tlen, dm)
```

```python
import functools
import math

import jax
import jax.numpy as jnp
from jax import lax
from jax.experimental import pallas as pl
from jax.experimental.pallas import tpu as pltpu

D_MODEL = 1024
HEAD_DIM = 64
CONV_WIDTH = 256
SB_WIDTH = 384
RWKV_WIDTH = 384
CONV_TAPS = 31
LORA = 32
BLOCK_Q = 128
RMS_EPS = 1e-6
LN_EPS = 1e-5
GN_EPS = 64e-5

LANES = 128
SUBLANES = 8
PAIRS = RWKV_WIDTH // LANES
CHUNK = 64
assert CHUNK == HEAD_DIM
RWKV_SEQS = 8
C_END = 3 * CONV_WIDTH
S_END = C_END + 4 * SB_WIDTH
R_IN = 4 * RWKV_WIDTH + 2 * LORA
D_IN = S_END + R_IN
GATE_OFF = 3 * RWKV_WIDTH + 2 * LORA
WA_WIDTH = LANES
SHIFT_COLS = 3 * RWKV_WIDTH + WA_WIDTH
VMEM_LIMIT = 56 * 1024 * 1024

F32 = jnp.float32
BF16 = jnp.bfloat16


def _dot1(a, b):
    return jnp.dot(a.astype(BF16), b.astype(BF16), preferred_element_type=F32)


def _sigmoid(x):
    return 0.5 * jnp.tanh(0.5 * x) + 0.5


def _silu(x):
    return x * _sigmoid(x)


CONV_HALO = 32
CONV_SUB = 32
MXU_COLS = 256


def _proj_slabs(h, w_ref, o_ref, cols=None):
    lo, hi = cols if cols is not None else (0, w_ref.shape[1])

    def slab(c0, c1):
        o_ref[:, c0 - lo:c1 - lo] = jnp.dot(h, w_ref[:, c0:c1], preferred_element_type=F32).astype(o_ref.dtype)

    return [functools.partial(slab, c0, min(c0 + MXU_COLS, hi)) for c0 in range(lo, hi, MXU_COLS)]


def _inproj_kernel(x_ref, g_ref, w_ref, dw_ref, dwb_ref, lng_ref, lnb_ref, pw_ref, pwb_ref,
                   yc_ref, qkv_ref, gs_ref, pr_ref, pc_ref, u_ref, *, tiles_per_seq):
    tt = x_ref.shape[0]

    @pl.when(pl.program_id(0) % tiles_per_seq == 0)
    def _():
        u_ref[0:CONV_HALO, :] = jnp.zeros((CONV_HALO, CONV_WIDTH), F32)

    x = x_ref[...]
    ms = jnp.mean(x * x, axis=-1, keepdims=True)
    h = (x * lax.rsqrt(ms + RMS_EPS) * g_ref[...]).astype(BF16)
    for job in _proj_slabs(h, w_ref, pc_ref, (0, C_END)):
        job()
    u_ref[CONV_HALO:CONV_HALO + tt, :] = pc_ref[:, 0:CONV_WIDTH] * _sigmoid(pc_ref[:, CONV_WIDTH:2 * CONV_WIDTH])
    slabs = (_proj_slabs(h, w_ref, qkv_ref, (C_END, C_END + 3 * SB_WIDTH))
             + _proj_slabs(h, w_ref, gs_ref, (C_END + 3 * SB_WIDTH, S_END)) + _proj_slabs(h, w_ref, pr_ref, (S_END, D_IN)))
    groups = tt // CONV_SUB
    for r0 in range(0, tt, CONV_SUB):
        gi = r0 // CONV_SUB
        for job in slabs[gi * len(slabs) // groups:(gi + 1) * len(slabs) // groups]:
            job()
        halves = []
        for l0 in range(0, CONV_WIDTH, LANES):
            ls = slice(l0, l0 + LANES)
            window = u_ref[r0:r0 + CONV_HALO + CONV_SUB, ls]
            acc = jnp.broadcast_to(dwb_ref[:, ls], (CONV_SUB, LANES))
            for b in range(SUBLANES):
                part = None
                for a in range((CONV_TAPS - 1 - b) // SUBLANES + 1):
                    tap = CONV_TAPS - 1 - (SUBLANES * a + b)
                    s = CONV_HALO - SUBLANES * (a + 1)
                    term = dw_ref[tap:tap + 1, ls] * window[s:s + CONV_SUB + SUBLANES, :]
                    part = term if part is None else part + term
                acc = acc + part[SUBLANES - b:SUBLANES - b + CONV_SUB, :]
            halves.append(acc)
        acc = jnp.concatenate(halves, axis=1)
        mean = jnp.mean(acc, axis=-1, keepdims=True)
        d = acc - mean
        var = jnp.mean(d * d, axis=-1, keepdims=True)
        y = _silu(d * lax.rsqrt(var + LN_EPS) * lng_ref[...] + lnb_ref[...])
        y = _dot1(y, pw_ref[...]) + pwb_ref[...]
        y = y * _silu(pc_ref[r0:r0 + CONV_SUB, 2 * CONV_WIDTH:3 * CONV_WIDTH])
        yc_ref[r0:r0 + CONV_SUB, :] = y.astype(yc_ref.dtype)
    u_ref[0:CONV_HALO, :] = u_ref[tt:tt + CONV_HALO, :]


def _layer_weight(w, layer):
    return pl.BlockSpec((None,) + w.shape[1:], lambda i: (layer, 0, 0))


def _inproj(x2, g, w, layer, conv_params, tm, tlen):
    n = x2.shape[0]
    full = lambda a: pl.BlockSpec(a.shape, lambda i: (0, 0))
    rows = lambda width: pl.BlockSpec((tm, width), lambda i: (i, 0))
    widths = (CONV_WIDTH, 3 * SB_WIDTH, SB_WIDTH, R_IN)
    dtypes = (BF16, BF16, F32, F32)
    return pl.pallas_call(
        functools.partial(_inproj_kernel, tiles_per_seq=tlen // tm),
        grid=(n // tm,),
        in_specs=[rows(D_MODEL), full(g), _layer_weight(w, layer)] + [full(a) for a in conv_params],
        out_specs=[rows(wd) for wd in widths],
        out_shape=[jax.ShapeDtypeStruct((n, wd), dt) for wd, dt in zip(widths, dtypes)],
        scratch_shapes=[pltpu.VMEM((tm, C_END), F32),
                        pltpu.VMEM((CONV_HALO + tm, CONV_WIDTH), F32)],
        compiler_params=pltpu.CompilerParams(dimension_semantics=("arbitrary",),
                                             vmem_limit_bytes=VMEM_LIMIT),
        name="inproj_conv",
    )(x2, g, w, *conv_params)


SB_DEAD_LOG = -150.0
SB_EAGER = 2


def _sb_kernel(q_ref, k_ref, v_ref, g_ref, o_ref, carry_ref, acc_ref):
    i = pl.program_id(1)
    bq = BLOCK_Q
    lane = lax.broadcasted_iota(jnp.int32, (bq, LANES), 1)
    head0 = lane < HEAD_DIM
    row = lax.broadcasted_iota(jnp.int32, (LANES, 2 * LANES), 0)
    col = lax.broadcasted_iota(jnp.int32, (LANES, 2 * LANES), 1)
    cs = jnp.where((col >= LANES) | (row > col), -1.0, 0.0).astype(BF16)
    pairs = range(PAIRS)
    sls = [slice(p * LANES, (p + 1) * LANES) for p in pairs]

    def stacked_q(p):
        q = q_ref[:, sls[p]].astype(F32) * (HEAD_DIM ** -0.5)
        return jnp.concatenate([jnp.where(head0, q, 0.0), jnp.where(head0, 0.0, q)], axis=0).astype(BF16)

    def process(js, first):
        starts = [pl.multiple_of(j * bq, bq) for j in js]
        items = [(n, p) for n in range(len(js)) for p in pairs]
        qs = [stacked_q(p) for p in pairs]
        zs = {(n, p): _mm_nt(qs[p], k_ref[pl.ds(starts[n], bq), sls[p]]) for n, p in items}
        if first:
            qi = lax.broadcasted_iota(jnp.int32, (2 * bq, bq), 0) % bq
            ki = lax.broadcasted_iota(jnp.int32, (2 * bq, bq), 1)
            causal = ki < qi
        log_betas, sums = {}, {}
        for n, p in items:
            z = zs[n, p]
            neg_log_keep = jnp.maximum(z, 0.0) + jnp.log(1.0 + jnp.exp(-jnp.abs(z)))
            log_betas[n, p] = z - neg_log_keep
            if first and n == 0:
                neg_log_keep = jnp.where(causal, neg_log_keep, 0.0)
            sums[n, p] = _dot1(neg_log_keep, cs)
        live = None
        pvs = {}
        for p in pairs:
            carry = None if first else carry_ref[p]
            for n in range(len(js)):
                rest = sums[n, p][:, :LANES]
                if carry is not None:
                    rest = rest + carry
                att = jnp.exp(log_betas[n, p] + rest)
                if first and n == 0:
                    att = jnp.where(causal, att, 0.0)
                row_sum = sums[n, p][:, LANES:]
                carry = row_sum if carry is None else carry + row_sum
                pvs[n, p] = _dot1(att, v_ref[pl.ds(starts[n], bq), sls[p]])
            carry_ref[p] = carry
            m = jnp.max(carry)
            live = m if live is None else jnp.maximum(live, m)
        for p in pairs:
            tot = pvs[0, p]
            for n in range(1, len(js)):
                tot = tot + pvs[n, p]
            if first:
                acc_ref[p] = tot
            else:
                acc_ref[p] += tot
        return live

    def eager(_):
        return jnp.int32(SB_EAGER), process([i - n for n in range(SB_EAGER + 1)], True)

    def diagonal_only(_):
        return jnp.int32(0), process([i], True)

    done, live = lax.cond(i >= SB_EAGER, eager, diagonal_only, None)

    def cond(c):
        return (c[0] < i) & (c[1] > SB_DEAD_LOG)

    def body(c):
        return c[0] + 1, process([i - 1 - c[0]], False)

    lax.while_loop(cond, body, (done, live))
    for p in range(PAIRS):
        sl = slice(p * LANES, (p + 1) * LANES)
        o = jnp.where(head0, acc_ref[p, :bq], acc_ref[p, bq:])
        o_ref[:, sl] = (o * _silu(g_ref[:, sl])).astype(o_ref.dtype)


def _sb(qkv, gate, bsz, tlen):
    nq = tlen // BLOCK_Q
    return pl.pallas_call(
        _sb_kernel,
        grid=(bsz, nq),
        in_specs=[pl.BlockSpec((BLOCK_Q, SB_WIDTH), lambda b, i: (b * nq + i, 0)),
                  pl.BlockSpec((tlen, SB_WIDTH), lambda b, i: (b, 1)),
                  pl.BlockSpec((tlen, SB_WIDTH), lambda b, i: (b, 2)),
                  pl.BlockSpec((BLOCK_Q, SB_WIDTH), lambda b, i: (b * nq + i, 0))],
        out_specs=pl.BlockSpec((BLOCK_Q, SB_WIDTH), lambda b, i: (b * nq + i, 0)),
        out_shape=jax.ShapeDtypeStruct((bsz * tlen, SB_WIDTH), BF16),
        scratch_shapes=[pltpu.VMEM((PAIRS, 2 * BLOCK_Q, LANES), F32),
                        pltpu.VMEM((PAIRS, 2 * BLOCK_Q, LANES), F32)],
        compiler_params=pltpu.CompilerParams(dimension_semantics=("parallel", "arbitrary"),
                                             vmem_limit_bytes=VMEM_LIMIT),
        name="stickbreak",
    )(qkv, qkv, qkv, gate)


def _stack_heads(a, head0):
    return jnp.concatenate([jnp.where(head0, a, 0.0), jnp.where(head0, 0.0, a)], axis=0)


def _rwkv_kernel(*refs, has_vres):
    if has_vres:
        (pr_ref, vf_ref, mu_ref, w0_ref, w2_ref, a0_ref, a2_ref, kks_ref, ka_ref, rk_ref, gng_ref, gnb_ref,
         v0_ref, v1_ref, v2_ref, y_ref, xs_ref, st_ref) = refs
    else:
        (pr_ref, mu_ref, w0_ref, w2_ref, a0_ref, a2_ref, kks_ref, ka_ref, rk_ref, gng_ref, gnb_ref,
         y_ref, vf_out_ref, xs_ref, st_ref) = refs
    c = CHUNK
    w = RWKV_WIDTH
    nb = pr_ref.shape[0]

    @pl.when(pl.program_id(1) == 0)
    def _():
        xs_ref[...] = jnp.zeros(xs_ref.shape, F32)
        st_ref[...] = jnp.zeros(st_ref.shape, F32)

    lane = lax.broadcasted_iota(jnp.int32, (c, LANES), 1)
    head0 = lane < HEAD_DIM
    ri = lax.broadcasted_iota(jnp.int32, (LANES, LANES), 0)
    ci = lax.broadcasted_iota(jnp.int32, (LANES, LANES), 1)
    same_head = (ri < HEAD_DIM) == (ci < HEAD_DIM)
    head_ones = jnp.where(same_head, 1.0, 0.0).astype(BF16)
    weights = dict(w2=w2_ref[...], a2=a2_ref[...])
    if has_vres:
        weights.update(v1=v1_ref[...], v2=v2_ref[...])
    ti = lax.broadcasted_iota(jnp.int32, (c, LANES), 0)
    ji = lane % HEAD_DIM
    lower = ti > ji
    lower_eq = ti >= ji
    eye = jnp.where(ti == ji, 1.0, 0.0)
    rc = lax.broadcasted_iota(jnp.int32, (c, c), 0)
    cc = lax.broadcasted_iota(jnp.int32, (c, c), 1)
    cum = jnp.where(rc >= cc, 1.0, 0.0).astype(BF16)

    def sequence(b):
        x = pr_ref[b, :, 0:SHIFT_COLS]
        first_row = lax.broadcasted_iota(jnp.int32, (c, SHIFT_COLS), 0) == 0
        prev = jnp.where(first_row, xs_ref[b, 0:1, :], pltpu.roll(x, 1, axis=0))
        xs_ref[b, 0:1, :] = x[c - 1:c, :]
        f = x + (prev - x) * mu_ref[...]
        r = f[:, 0:w]
        k = f[:, w:2 * w]
        v = f[:, 2 * w:3 * w]
        wa = f[:, 3 * w:3 * w + WA_WIDTH]
        lora = [(jnp.tanh(wa).astype(BF16), weights["w2"]), (wa.astype(BF16), weights["a2"])]
        if has_vres:
            lora.append((v.astype(BF16), weights["v1"]))
        products = yield lora
        logw = (-math.exp(-0.5)) * _sigmoid(w0_ref[...] + products[0])
        a_lr = _sigmoid(a0_ref[...] + products[1])
        logp = _mm(cum, logw.astype(BF16))
        if has_vres:
            (v_up,) = yield [(products[2].astype(BF16), weights["v2"])]
            v = v + (vf_ref[b] - v) * _sigmoid(v0_ref[...] + v_up)
        else:
            vf_out_ref[b] = v
            yield []
        p = jnp.exp(logp)
        chunk_start = lax.broadcasted_iota(jnp.int32, (c, w), 0) == 0
        p_prev = jnp.where(chunk_start, 1.0, pltpu.roll(p, 1, axis=0))
        inv_p = jnp.exp(-logp)
        kk = k * kks_ref[...]
        k2 = k * (1.0 + (a_lr - 1.0) * ka_ref[...])
        sls = [slice(pr * LANES, (pr + 1) * LANES) for pr in range(PAIRS)]
        n2s = yield [((kk[:, sl] * kk[:, sl]).astype(BF16), head_ones) for sl in sls]
        stack = lambda a: _stack_heads(a, head0).astype(BF16)
        chains = []
        for pr, sl in enumerate(sls):
            kk_p = kk[:, sl] * lax.rsqrt(jnp.maximum(n2s[pr], 1e-24))
            a_n = -kk_p * p_prev[:, sl]
            b_n = kk_p * a_lr[:, sl] * inv_p[:, sl]
            k_n = k2[:, sl] * inv_p[:, sl]
            r_n = r[:, sl] * p[:, sl]
            pe = p[c - 1:c, sl]
            ar = jnp.concatenate([a_n, r_n], axis=0).astype(BF16)
            ch = dict(sl=sl, idx=b * PAIRS + pr, ar=ar, pe=pe, v_st=stack(v[:, sl]),
                      be_ke=jnp.concatenate([(b_n * pe).T, (k_n * pe).T], axis=1).astype(BF16),
                      rkk=(r[:, sl] * k2[:, sl] * rk_ref[:, sl]).astype(BF16), v=v[:, sl])
            ch["gram"] = _mm_nt(ar, jnp.concatenate([stack(b_n), stack(k_n)], axis=0))
            chains.append(ch)
        yield []
        requests = []
        for ch in chains:
            gram = ch.pop("gram")
            ab = jnp.where(lower, gram[:c, :LANES], 0.0)
            ak = jnp.where(lower, gram[:c, LANES:], 0.0).astype(BF16)
            ch["rb_rk"] = jnp.concatenate([jnp.where(lower_eq, gram[c:, :LANES], 0.0),
                                           jnp.where(lower_eq, gram[c:, LANES:], 0.0)], axis=1).astype(BF16)
            g = st_ref[ch["idx"]]
            ch["g"] = g
            ch["s"] = eye + ab
            requests += [(ak, ch["v_st"]), (ch["ar"], g.astype(BF16)), (ab.astype(BF16), stack(ab))]
        products = yield requests
        for i, ch in enumerate(chains):
            ch["akv"], ch["ar_g"], ch["pw"] = products[3 * i:3 * i + 3]
        for _ in range(4):
            for ch in chains:
                res = _mm(ch["pw"].astype(BF16), jnp.concatenate([stack(ch["s"]), stack(ch["pw"])], axis=1))
                ch["s"] = ch["s"] + res[:, :LANES]
                ch["pw"] = res[:, LANES:]
            yield []
        products = yield [(ch["pw"].astype(BF16), stack(ch["s"])) for ch in chains]
        for ch, prod in zip(chains, products):
            ch["s"] = ch["s"] + prod
        products = yield [(ch["s"].astype(BF16), stack(ch["ar_g"][:c] + ch["akv"])) for ch in chains]
        requests = []
        for ch, u in zip(chains, products):
            ch["y_in"] = _mm(ch["rb_rk"], jnp.concatenate([stack(u), ch["v_st"]], axis=0))
            requests += [(ch["be_ke"], jnp.concatenate([u.astype(BF16), ch["v"].astype(BF16)], axis=0)),
                         (ch["rkk"], head_ones)]
        products = yield requests
        requests = []
        for i, ch in enumerate(chains):
            inc, ch["bonus"] = products[2 * i:2 * i + 2]
            pcol = jnp.broadcast_to(ch["pe"], (LANES, LANES)).T
            st_ref[ch["idx"]] = pcol * ch["g"] + jnp.where(same_head, inc, 0.0)
            ch["y"] = ch["ar_g"][c:] + ch["y_in"]
            requests.append((ch["y"].astype(BF16), head_ones))
        means = yield requests
        for ch, mean in zip(chains, means):
            ch["d"] = ch["y"] - mean * (1.0 / HEAD_DIM)
        variances = yield [((ch["d"] * ch["d"]).astype(BF16), head_ones) for ch in chains]
        for ch, var in zip(chains, variances):
            sl = ch["sl"]
            y = ch["d"] * lax.rsqrt(var * (1.0 / HEAD_DIM) + GN_EPS) * gng_ref[:, sl] + gnb_ref[:, sl]
            y = y + ch["bonus"] * ch["v"]
            y = y * _silu(pr_ref[b, :, GATE_OFF + sl.start:GATE_OFF + sl.stop])
            y_ref[b, :, sl] = y.astype(y_ref.dtype)

    programs = [sequence(b) for b in range(nb)]
    requests = [next(prog) for prog in programs]
    while programs:
        products = [[None] * len(reqs) for reqs in requests]
        by_right = {}
        for j, reqs in enumerate(requests):
            for i, (left, right) in enumerate(reqs):
                by_right.setdefault(id(right), (right, []))[1].append((j, i, left))
        for right, users in by_right.values():
            tall = _mm(jnp.concatenate([left for _, _, left in users], axis=0), right)
            row = 0
            for j, i, left in users:
                products[j][i] = tall[row:row + left.shape[0]]
                row += left.shape[0]
        alive, requests = [], []
        for prog, prods in zip(programs, products):
            try:
                requests.append(prog.send(prods))
                alive.append(prog)
            except StopIteration:
                pass
        programs = alive


def _mm(a, b):
    return jnp.dot(a, b, preferred_element_type=F32)


def _mm_nt(a, b):
    return lax.dot_general(a, b, (((1,), (1,)), ((), ())), preferred_element_type=F32)


def _rwkv(pr, vfirst, mu, w0, w2, a0, a2, kks, ka, rk, gng, gnb, vres, bsz, tlen):
    nc = tlen // CHUNK
    nb = RWKV_SEQS if bsz % RWKV_SEQS == 0 else 1
    has_vres = vres is not None
    full = lambda a: pl.BlockSpec(a.shape, lambda b, t: (0,) * a.ndim)
    row_spec = lambda width: pl.BlockSpec((nb, CHUNK, width), lambda b, t: (b, t, 0))
    params = [mu, w0, w2, a0, a2, kks, ka, rk, gng, gnb]
    in_specs = [row_spec(R_IN)]
    args = [pr]
    if has_vres:
        in_specs.append(row_spec(RWKV_WIDTH))
        args.append(vfirst)
        params = params + list(vres)
    in_specs += [full(a) for a in params]
    args += params
    y_shape = jax.ShapeDtypeStruct((bsz, tlen, RWKV_WIDTH), BF16)
    if has_vres:
        out_specs, out_shape = row_spec(RWKV_WIDTH), y_shape
    else:
        vf_shape = jax.ShapeDtypeStruct((bsz, tlen, RWKV_WIDTH), F32)
        out_specs, out_shape = [row_spec(RWKV_WIDTH), row_spec(RWKV_WIDTH)], [y_shape, vf_shape]
    return pl.pallas_call(
        functools.partial(_rwkv_kernel, has_vres=has_vres),
        grid=(bsz // nb, nc),
        in_specs=in_specs,
        out_specs=out_specs,
        out_shape=out_shape,
        scratch_shapes=[pltpu.VMEM((nb, SUBLANES, SHIFT_COLS), F32),
                        pltpu.VMEM((nb * PAIRS, LANES, LANES), F32)],
        compiler_params=pltpu.CompilerParams(dimension_semantics=("parallel", "arbitrary"),
                                             vmem_limit_bytes=VMEM_LIMIT),
        name="rwkv7_vres" if has_vres else "rwkv7",
    )(*args)


OUT_SUB = 256


def _outproj_kernel(x_ref, yc_ref, ys_ref, yr_ref, w_ref, g_ref, o_ref):
    groups = [slice(r0, r0 + OUT_SUB) for r0 in range(0, x_ref.shape[0], OUT_SUB)]
    mixes = [jnp.dot(jnp.concatenate([yc_ref[rs, :], ys_ref[rs, :], yr_ref[rs, :]], axis=1), w_ref[...],
                     preferred_element_type=F32) for rs in groups]
    for rs, m in zip(groups, mixes):
        ms = jnp.mean(m * m, axis=-1, keepdims=True)
        o_ref[rs, :] = x_ref[rs, :] + m * lax.rsqrt(ms + RMS_EPS) * g_ref[...]


def _outproj(x2, yc, ys, yr, w, layer, g, tm):
    n = x2.shape[0]
    full = lambda a: pl.BlockSpec(a.shape, lambda i: (0, 0))
    rows = lambda a: pl.BlockSpec((tm, a.shape[1]), lambda i: (i, 0))
    return pl.pallas_call(
        _outproj_kernel,
        grid=(n // tm,),
        in_specs=[rows(x2), rows(yc), rows(ys), rows(yr), _layer_weight(w, layer), full(g)],
        out_specs=rows(x2),
        out_shape=jax.ShapeDtypeStruct(x2.shape, F32),
        compiler_params=pltpu.CompilerParams(dimension_semantics=("parallel",),
                                             vmem_limit_bytes=VMEM_LIMIT),
        name="outproj",
    )(x2, yc, ys, yr, w, g)


def _pad_rows(a, rows):
    return jnp.concatenate([a, jnp.zeros((rows - a.shape[0],) + a.shape[1:], a.dtype)], axis=0)


def kernel(x, pre_norm_g, post_norm_g, w_in, w_out, conv_dw, conv_dw_b, conv_ln_g, conv_ln_b, conv_pw, conv_pw_b, rwkv_mu, rwkv_w0, rwkv_w2, rwkv_a0, rwkv_a2, rwkv_kk_scale, rwkv_ka, rwkv_rk, rwkv_gn_g, rwkv_gn_b, rwkv_v0, rwkv_v1, rwkv_v2):
    bsz, tlen, dm = x.shape
    depth = w_in.shape[0]
    assert dm == D_MODEL and tlen % BLOCK_Q == 0 and tlen % CHUNK == 0
    n = bsz * tlen
    tm = 256 if tlen % 256 == 0 else BLOCK_Q
    tm_out = next(g * OUT_SUB for g in (4, 2, 1) if n % (g * OUT_SUB) == 0)
    row = lambda a: a.reshape(1, -1)
    assert w_in.shape[2] == D_IN
    w_in_b, w_out_b = w_in.astype(BF16), w_out.astype(BF16)
    x2 = x.reshape(n, dm)
    vfirst = None
    for l in range(depth):
        conv_params = (conv_dw[l], row(conv_dw_b[l]), row(conv_ln_g[l]), row(conv_ln_b[l]),
                       conv_pw[l].astype(BF16), row(conv_pw_b[l]))
        yc, qkv, g_sb, pr = _inproj(x2, row(pre_norm_g[l]), w_in_b, l, conv_params, tm, tlen)
        ys = _sb(qkv, g_sb, bsz, tlen)

        mu = jnp.concatenate([rwkv_mu[l], jnp.zeros((WA_WIDTH - 2 * LORA,), F32)]).reshape(1, -1)
        w2 = _pad_rows(rwkv_w2[l], WA_WIDTH).astype(BF16)
        a2 = _pad_rows(jnp.concatenate([jnp.zeros((LORA, RWKV_WIDTH), F32), rwkv_a2[l]], axis=0),
                       WA_WIDTH).astype(BF16)
        vres = None
        if l > 0:
            v1 = jnp.concatenate([rwkv_v1[l - 1], jnp.zeros((RWKV_WIDTH, LANES - LORA), F32)], axis=1)
            vres = (row(rwkv_v0[l - 1]), v1.astype(BF16), _pad_rows(rwkv_v2[l - 1], LANES).astype(BF16))
        res = _rwkv(pr.reshape(bsz, tlen, R_IN), vfirst, mu, row(rwkv_w0[l]), w2, row(rwkv_a0[l]), a2,
                    row(rwkv_kk_scale[l]), row(rwkv_ka[l]), row(rwkv_rk[l]), row(rwkv_gn_g[l]),
                    row(rwkv_gn_b[l]), vres, bsz, tlen)
        if l == 0:
            yr, vfirst = res
        else:
            yr = res
        yr = yr.reshape(n, RWKV_WIDTH)

        x2 = _outproj(x2, yc, ys, yr, w_out_b, l, row(post_norm_g[l]), tm_out)
    return x2.reshape(bsz, tlen, dm)
```
